```python
import math
import jax, jax.numpy as jnp
from jax import lax
import numpy as np

D_MODEL = 1024
BATCH = 32
SEQ = 2048
DEPTH = 2

GRID_W = 64
CTX_LEN = 256
HEAD_DIM = 64
A_HEADS = D_MODEL // (2 * HEAD_DIM)
A_WIDTH = A_HEADS * 2 * HEAD_DIM
A_IN_WIDTH = 4 * A_WIDTH
B_HEADS = D_MODEL // HEAD_DIM
B_KV_HEADS = 4
B_GROUP = B_HEADS // B_KV_HEADS
B_WIDTH = B_HEADS * HEAD_DIM
B_KV_WIDTH = B_KV_HEADS * HEAD_DIM
B_IN_WIDTH = 2 * B_WIDTH + 2 * B_KV_WIDTH
MIX_WIDTH = A_WIDTH
WINDOW = 128
Q_BLOCK = 128
BAND = Q_BLOCK + 2 * WINDOW
ROPE_THETA = 10000.0
NORM_EPS = 1e-6
SUBLN_EPS = 1e-5
NEG_INF = -1e30
ATTN_SCALE = HEAD_DIM ** -0.5
N_A_LAYERS = (DEPTH + 1) // 2
N_B_LAYERS = DEPTH // 2

kernel_name = 'hybrid_diffattn_windowgqa_ctxprefix_dit'


def rms_norm(x, g, eps=NORM_EPS):
    x32 = x.astype(jnp.float32)
    y = x32 * lax.rsqrt(jnp.mean(x32 * x32, axis=-1, keepdims=True) + eps)
    return (y * g.astype(jnp.float32)).astype(x.dtype)


def modulate(h, shift, scale):
    return h * (1.0 + scale) + shift


def axial_angles(rows, rot_dim=HEAD_DIM):
    row = jnp.repeat(jnp.arange(rows, dtype=jnp.int32), GRID_W).astype(jnp.float32)
    col = jnp.tile(jnp.arange(GRID_W, dtype=jnp.int32), rows).astype(jnp.float32)
    axis_dim = rot_dim // 2
    inv_freq = ROPE_THETA ** (-jnp.arange(0, axis_dim, 2, dtype=jnp.float32) / axis_dim)
    return row[:, None] * inv_freq, col[:, None] * inv_freq


def rope_1d(x, ang):
    ang = ang.reshape(ang.shape[:1] + (1,) * (x.ndim - 3) + ang.shape[1:])
    cos, sin = jnp.cos(ang), jnp.sin(ang)
    x1, x2 = jnp.split(x.astype(jnp.float32), 2, axis=-1)
    return jnp.concatenate([x1 * cos - x2 * sin, x2 * cos + x1 * sin], axis=-1).astype(x.dtype)


def rope_axial(x, ang_r, ang_c):
    xr, xc = jnp.split(x, 2, axis=-1)
    return jnp.concatenate([rope_1d(xr, ang_r), rope_1d(xc, ang_c)], axis=-1)


def diff_attention_mixer(hx, hc, w_in, lq1, lk1, lq2, lk2, subln_g, layer_idx, with_ctx_out, ang_r, ang_c):
    B, S, _ = hx.shape
    nblk = S // Q_BLOCK
    lam_init = 0.8 - 0.6 * math.exp(-0.3 * layer_idx)
    f32 = jnp.float32
    lam = (jnp.exp(jnp.sum(lq1.astype(f32) * lk1.astype(f32)))
           - jnp.exp(jnp.sum(lq2.astype(f32) * lk2.astype(f32))) + lam_init)

    def split(p):
        q, k, v, g = jnp.split(p, [A_WIDTH, 2 * A_WIDTH, 3 * A_WIDTH], axis=-1)
        lead = p.shape[:2]
        return (q.reshape(lead + (A_HEADS, 2, HEAD_DIM)),
                k.reshape(lead + (A_HEADS, 2, HEAD_DIM)),
                v.reshape(lead + (A_HEADS, 2 * HEAD_DIM)), g)

    qx, kx, vx, gx = split(hx @ w_in)
    qc, kc, vc, gc = split(hc @ w_in)
    qx = rope_axial(qx, ang_r, ang_c)
    kx = rope_axial(kx, ang_r, ang_c)
    k_all = jnp.concatenate([kc, kx], axis=1)
    v_all = jnp.concatenate([vc, vx], axis=1)

    def attend(qb, keys, vals):
        s = jnp.einsum('bqhmd,bkhmd->bhmqk', qb, keys).astype(f32) * ATTN_SCALE
        p = jax.nn.softmax(s, axis=-1)
        a = p[:, :, 0] - lam * p[:, :, 1]
        return jnp.einsum('bhqk,bkhe->bqhe', a.astype(vals.dtype), vals)

    q_blocks = qx.reshape(B, nblk, Q_BLOCK, A_HEADS, 2, HEAD_DIM).swapaxes(0, 1)
    ox = lax.map(lambda qb: attend(qb, k_all, v_all), q_blocks)
    ox = ox.swapaxes(0, 1).reshape(B, S, A_HEADS, 2 * HEAD_DIM)

    def finish(o, g):
        o = rms_norm(o, subln_g, SUBLN_EPS) * (1.0 - lam_init)
        return o.reshape(o.shape[:2] + (A_WIDTH,)) * jax.nn.silu(g)

    out_x = finish(ox, gx)
    out_c = finish(attend(qc, kc, vc), gc) if with_ctx_out else None
    return out_x, out_c


def window_gqa_mixer(hx, hc, w_in, sink, with_ctx_out, ang_r, ang_c):
    B, S, _ = hx.shape
    nblk = S // Q_BLOCK
    f32 = jnp.float32

    def split(p):
        q, k, v, g = jnp.split(p, [B_WIDTH, B_WIDTH + B_KV_WIDTH, B_WIDTH + 2 * B_KV_WIDTH], axis=-1)
        lead = p.shape[:2]
        return (q.reshape(lead + (B_KV_HEADS, B_GROUP, HEAD_DIM)),
                k.reshape(lead + (B_KV_HEADS, HEAD_DIM)),
                v.reshape(lead + (B_KV_HEADS, HEAD_DIM)), g)

    qx, kx, vx, gx = split(hx @ w_in)
    qc, kc, vc, gc = split(hc @ w_in)
    qx = rope_axial(qx, ang_r, ang_c)
    kx = rope_axial(kx, ang_r, ang_c)
    n_ctx = kc.shape[1]
    sink_f = sink.astype(f32).reshape(B_KV_HEADS, B_GROUP)[None, :, :, None, None]

    def sink_softmax(s):
        sb = jnp.broadcast_to(sink_f, s.shape[:-1] + (1,))
        return jax.nn.softmax(jnp.concatenate([s, sb], axis=-1), axis=-1)[..., :-1]

    pad = ((0, 0), (WINDOW, WINDOW), (0, 0), (0, 0))
    kp = jnp.pad(kx, pad)
    vp = jnp.pad(vx, pad)
    q_blocks = qx.reshape(B, nblk, Q_BLOCK, B_KV_HEADS, B_GROUP, HEAD_DIM).swapaxes(0, 1)

    def block(args):
        qb, i = args
        start = i * Q_BLOCK
        kb = lax.dynamic_slice_in_dim(kp, start, BAND, axis=1)
        vb = lax.dynamic_slice_in_dim(vp, start, BAND, axis=1)
        qpos = start + jnp.arange(Q_BLOCK)
        kpos = start - WINDOW + jnp.arange(BAND)
        mask = ((jnp.abs(qpos[:, None] - kpos[None, :]) <= WINDOW)
                & (kpos >= 0)[None, :] & (kpos < S)[None, :])
        s_band = jnp.einsum('bqhgd,bjhd->bhgqj', qb, kb).astype(f32) * ATTN_SCALE
        s_band = jnp.where(mask, s_band, NEG_INF)
        s_ctx = jnp.einsum('bqhgd,bjhd->bhgqj', qb, kc).astype(f32) * ATTN_SCALE
        p = sink_softmax(jnp.concatenate([s_ctx, s_band], axis=-1)).astype(vb.dtype)
        return (jnp.einsum('bhgqj,bjhd->bqhgd', p[..., :n_ctx], vc)
                + jnp.einsum('bhgqj,bjhd->bqhgd', p[..., n_ctx:], vb))

    ox = lax.map(block, (q_blocks, jnp.arange(nblk)))
    out_x = ox.swapaxes(0, 1).reshape(B, S, B_WIDTH) * jax.nn.silu(gx)
    out_c = None
    if with_ctx_out:
        s = jnp.einsum('bqhgd,bjhd->bhgqj', qc, kc).astype(f32) * ATTN_SCALE
        p = sink_softmax(s).astype(vc.dtype)
        oc = jnp.einsum('bhgqj,bjhd->bqhgd', p, vc)
        out_c = oc.reshape(oc.shape[:2] + (B_WIDTH,)) * jax.nn.silu(gc)
    return out_x, out_c


def setup_inputs(seed: int = 0) -> dict:
    key = jax.random.key(seed)
    ks = jax.random.split(key, 17)
    nrm = jax.random.normal
    f32 = jnp.float32
    return {
        'x': nrm(ks[0], (BATCH, SEQ, D_MODEL), f32),
        'c': nrm(ks[1], (BATCH, D_MODEL), f32),
        'ctx': nrm(ks[2], (BATCH, CTX_LEN, D_MODEL), f32),
        'c_ctx': nrm(ks[3], (D_MODEL,), f32),
        'w_mod': nrm(ks[4], (DEPTH, D_MODEL, 3 * D_MODEL), f32) * D_MODEL ** -0.5,
        'b_mod': 0.01 * nrm(ks[5], (DEPTH, 3 * D_MODEL), f32),
        'norm_g': 1.0 + 0.02 * nrm(ks[6], (DEPTH, D_MODEL), f32),
        'w_o': nrm(ks[7], (DEPTH, MIX_WIDTH, D_MODEL), f32) * MIX_WIDTH ** -0.5,
        'a_w_in': nrm(ks[8], (N_A_LAYERS, D_MODEL, A_IN_WIDTH), f32) * D_MODEL ** -0.5,
        'a_lambda_q1': 0.1 * nrm(ks[9], (N_A_LAYERS, HEAD_DIM), f32),
        'a_lambda_k1': 0.1 * nrm(ks[10], (N_A_LAYERS, HEAD_DIM), f32),
        'a_lambda_q2': 0.1 * nrm(ks[11], (N_A_LAYERS, HEAD_DIM), f32),
        'a_lambda_k2': 0.1 * nrm(ks[12], (N_A_LAYERS, HEAD_DIM), f32),
        'a_subln_g': 1.0 + 0.02 * nrm(ks[13], (N_A_LAYERS, 2 * HEAD_DIM), f32),
        'b_w_in': nrm(ks[14], (N_B_LAYERS, D_MODEL, B_IN_WIDTH), f32) * D_MODEL ** -0.5,
        'b_sink': 0.5 * nrm(ks[15], (N_B_LAYERS, B_HEADS), f32),
        'final_g': 1.0 + 0.02 * nrm(ks[16], (D_MODEL,), f32),
    }


def reference(x, c, ctx, c_ctx, w_mod, b_mod, norm_g, w_o, a_w_in, a_lambda_q1, a_lambda_k1,
              a_lambda_q2, a_lambda_k2, a_subln_g, b_w_in, b_sink, final_g):
    n_tokens = x.shape[1]
    rows = n_tokens // GRID_W
    ang_r, ang_c = axial_angles(rows)
    sc = jax.nn.silu(c)
    sctx = jax.nn.silu(c_ctx)
    for i in range(DEPTH):
        last = i == DEPTH - 1
        shift_x, scale_x, gate_x = jnp.split(sc @ w_mod[i] + b_mod[i], 3, axis=-1)
        shift_c, scale_c, gate_c = jnp.split(sctx @ w_mod[i] + b_mod[i], 3, axis=-1)
        hx = modulate(rms_norm(x, norm_g[i]), shift_x[:, None], scale_x[:, None])
        hc = modulate(rms_norm(ctx, norm_g[i]), shift_c, scale_c)
        j = i // 2
        if i % 2 == 0:
            ox, oc = diff_attention_mixer(hx, hc, a_w_in[j], a_lambda_q1[j], a_lambda_k1[j],
                                          a_lambda_q2[j], a_lambda_k2[j], a_subln_g[j], i,
                                          not last, ang_r, ang_c)
        else:
            ox, oc = window_gqa_mixer(hx, hc, b_w_in[j], b_sink[j], not last, ang_r, ang_c)
        x = x + gate_x[:, None] * (ox @ w_o[i])
        if not last:
            ctx = ctx + gate_c * (oc @ w_o[i])
    return rms_norm(x, final_g)
```

```python
import functools
import math

import jax
import jax.numpy as jnp
from jax import lax
from jax.experimental import pallas as pl
from jax.experimental.pallas import tpu as pltpu

D_MODEL = 1024
GRID_W = 64
HEAD_DIM = 64
LANES = 128
A_HEADS = D_MODEL // (2 * HEAD_DIM)
B_HEADS = D_MODEL // HEAD_DIM
B_KV_HEADS = 4
B_GROUP = B_HEADS // B_KV_HEADS
B_SLAB = B_GROUP * HEAD_DIM
WINDOW = 128
Q_BLOCK = 128
BAND = Q_BLOCK + 2 * WINDOW
ROPE_THETA = 10000.0
NORM_EPS = 1e-6
SUBLN_EPS = 1e-5
NEG_INF = -1e30
ATTN_SCALE = HEAD_DIM ** -0.5
MOD_ROWS = 40
VMEM_LIMIT = 56 * 1024 * 1024

F32 = jnp.float32
BF16 = jnp.bfloat16
NT_DIMS = (((1,), (1,)), ((), ()))


def _params(n_grid):
    return pltpu.CompilerParams(
        dimension_semantics=("arbitrary",) * n_grid,
        vmem_limit_bytes=VMEM_LIMIT)


def _silu(x):
    return x * (1.0 / (1.0 + jnp.exp(-x)))


def _mod_kernel(act_ref, w_ref, b_ref, out_ref):
    a = _silu(act_ref[...])
    out_ref[...] = jnp.dot(a, w_ref[...], preferred_element_type=F32,
                           precision=lax.Precision.HIGHEST) + b_ref[...]


def _mod_call(act, w_mod, b_mod):
    depth = w_mod.shape[0]
    n_blk = w_mod.shape[2] // D_MODEL
    return pl.pallas_call(
        _mod_kernel,
        grid=(depth, n_blk),
        in_specs=[
            pl.BlockSpec((MOD_ROWS, D_MODEL), lambda l, n: (0, 0)),
            pl.BlockSpec((None, D_MODEL, D_MODEL), lambda l, n: (l, 0, n)),
            pl.BlockSpec((None, 1, D_MODEL), lambda l, n: (l, 0, n)),
        ],
        out_specs=pl.BlockSpec((None, MOD_ROWS, D_MODEL), lambda l, n: (l, 0, n)),
        out_shape=jax.ShapeDtypeStruct((depth, MOD_ROWS, 3 * D_MODEL), F32),
        compiler_params=_params(2),
        name="mod_vectors",
    )(act, w_mod, b_mod.reshape(depth, 1, 3 * D_MODEL))


def _rope_block(r, cos, sin_a, sin_b):
    return (r * cos + pltpu.roll(r, LANES - 16, 1) * sin_a
            + pltpu.roll(r, 16, 1) * sin_b)


def _proj_kernel(x_ref, mod_ref, g_ref, w_ref, cos_ref, sa_ref, sb_ref, *out_refs,
                 kinds, rope):
    x = x_ref[...]
    ms = jnp.mean(x * x, axis=-1, keepdims=True)
    y = x * lax.rsqrt(ms + NORM_EPS) * g_ref[...]
    shift = mod_ref[:, 0:D_MODEL]
    scale = mod_ref[:, D_MODEL:2 * D_MODEL]
    h = (y * (1.0 + scale) + shift).astype(BF16)
    for n, (kind, out_ref) in enumerate(zip(kinds, out_refs)):
        r = jnp.dot(h, w_ref[:, n * D_MODEL:(n + 1) * D_MODEL],
                    preferred_element_type=F32)
        if kind == "q":
            r = r * ATTN_SCALE
        if rope and kind in ("q", "k"):
            cos, sin_a, sin_b = cos_ref[...], sa_ref[...], sb_ref[...]
            for hb in range(D_MODEL // LANES):
                sl = slice(hb * LANES, (hb + 1) * LANES)
                out_ref[:, sl] = _rope_block(r[:, sl], cos, sin_a, sin_b).astype(BF16)
        else:
            out_ref[...] = r.astype(BF16)


def _proj_call(x, mod, norm_g, w, tables, kinds, rope, tm, name):
    bsz, t, _ = x.shape
    per_batch_mod = mod.shape[0] != 1
    mod_map = (lambda b, i: (b, 0, 0)) if per_batch_mod else (lambda b, i: (0, 0, 0))
    n_out = len(kinds)
    return pl.pallas_call(
        functools.partial(_proj_kernel, kinds=kinds, rope=rope),
        grid=(bsz, t // tm),
        in_specs=[
            pl.BlockSpec((None, tm, D_MODEL), lambda b, i: (b, i, 0)),
            pl.BlockSpec((None, 1, 3 * D_MODEL), mod_map),
            pl.BlockSpec((1, D_MODEL), lambda b, i: (0, 0)),
            pl.BlockSpec((D_MODEL, n_out * D_MODEL), lambda b, i: (0, 0)),
            pl.BlockSpec((tm, LANES), lambda b, i: (i, 0)),
            pl.BlockSpec((tm, LANES), lambda b, i: (i, 0)),
            pl.BlockSpec((tm, LANES), lambda b, i: (i, 0)),
        ],
        out_specs=[pl.BlockSpec((None, tm, D_MODEL), lambda b, i: (b, i, 0))] * n_out,
        out_shape=[jax.ShapeDtypeStruct((bsz, t, D_MODEL), BF16)] * n_out,
        compiler_params=_params(2),
        name=name,
    )(x, mod, norm_g.reshape(1, D_MODEL), w, *tables)


def _diff_attn_kernel(*refs, with_latent, lam_init):
    if with_latent:
        (q_ref, kc_ref, vc_ref, kx_ref, vx_ref, gate_ref, lq1_ref, lk1_ref, lq2_ref,
         lk2_ref, sg_ref, o_ref) = refs
    else:
        (q_ref, kc_ref, vc_ref, gate_ref, lq1_ref, lk1_ref, lq2_ref, lk2_ref, sg_ref,
         o_ref) = refs
    q = q_ref[...]
    tq = q.shape[0]
    lane = lax.broadcasted_iota(jnp.int32, q.shape, 1)
    zero = jnp.zeros_like(q)
    qs = jnp.concatenate([jnp.where(lane < HEAD_DIM, q, zero),
                          jnp.where(lane >= HEAD_DIM, q, zero)], axis=0)
    s_c = lax.dot_general(qs, kc_ref[...], NT_DIMS, preferred_element_type=F32)
    m = jnp.max(s_c, axis=-1, keepdims=True)
    if with_latent:
        s_x = lax.dot_general(qs, kx_ref[...], NT_DIMS, preferred_element_type=F32)
        m = jnp.maximum(m, jnp.max(s_x, axis=-1, keepdims=True))
    e_c = jnp.exp(s_c - m)
    l = jnp.sum(e_c, axis=-1, keepdims=True)
    if with_latent:
        e_x = jnp.exp(s_x - m)
        l = l + jnp.sum(e_x, axis=-1, keepdims=True)
    lam = (jnp.exp(jnp.sum(lq1_ref[...] * lk1_ref[...], axis=-1, keepdims=True))
           - jnp.exp(jnp.sum(lq2_ref[...] * lk2_ref[...], axis=-1, keepdims=True))
           + lam_init)
    row = lax.broadcasted_iota(jnp.int32, l.shape, 0)
    inv_l = 1.0 / l
    coef = jnp.where(row < tq, inv_l, -lam * inv_l)
    w_c = e_c * coef
    a_c = (w_c[:tq] + w_c[tq:]).astype(BF16)
    o = jnp.dot(a_c, vc_ref[...], preferred_element_type=F32)
    if with_latent:
        w_x = e_x * coef
        a_x = (w_x[:tq] + w_x[tq:]).astype(BF16)
        o = o + jnp.dot(a_x, vx_ref[...], preferred_element_type=F32)
    ms = jnp.mean(o * o, axis=-1, keepdims=True)
    o = o * lax.rsqrt(ms + SUBLN_EPS) * sg_ref[...] * (1.0 - lam_init)
    o_ref[...] = (o * _silu(gate_ref[...].astype(F32))).astype(BF16)


def _diff_attn_call(q, kc, vc, kx, vx, gate, lams, subln_g, lam_init, tq, name):
    bsz, t, _ = q.shape
    n_ctx = kc.shape[1]
    with_latent = kx is not None
    head_blk = lambda rows: pl.BlockSpec((None, rows, LANES), lambda b, h, i: (b, 0, h))
    tile_blk = pl.BlockSpec((None, tq, LANES), lambda b, h, i: (b, i, h))
    vec_blk = lambda n: pl.BlockSpec((1, n), lambda b, h, i: (0, 0))
    in_specs = [tile_blk, head_blk(n_ctx), head_blk(n_ctx)]
    args = [q, kc, vc]
    if with_latent:
        in_specs += [head_blk(kx.shape[1]), head_blk(kx.shape[1])]
        args += [kx, vx]
    in_specs += [tile_blk] + [vec_blk(HEAD_DIM)] * 4 + [vec_blk(2 * HEAD_DIM)]
    args += [gate] + [v.reshape(1, HEAD_DIM) for v in lams] + [subln_g.reshape(1, -1)]
    return pl.pallas_call(
        functools.partial(_diff_attn_kernel, with_latent=with_latent, lam_init=lam_init),
        grid=(bsz, A_HEADS, t // tq),
        in_specs=in_specs,
        out_specs=tile_blk,
        out_shape=jax.ShapeDtypeStruct((bsz, t, D_MODEL), BF16),
        compiler_params=_params(3),
        name=name,
    )(*args)


def _win_attn_kernel(sink_ref, q_ref, kc_ref, vc_ref, kx_ref, vx_ref, gate_ref, o_ref):
    i = pl.program_id(1)
    tq = q_ref.shape[0]
    seq = kx_ref.shape[0]
    start = i * tq
    kstart = pl.multiple_of(jnp.clip(start - WINDOW, 0, seq - BAND), WINDOW)
    rows = B_GROUP * tq
    row = lax.broadcasted_iota(jnp.int32, (rows, BAND), 0)
    col = lax.broadcasted_iota(jnp.int32, (rows, BAND), 1)
    qpos = start + (row & (tq - 1))
    kpos = kstart + col
    band_ok = jnp.abs(qpos - kpos) <= WINDOW
    row1 = lax.broadcasted_iota(jnp.int32, (rows, 1), 0)
    lane_grp = lax.broadcasted_iota(jnp.int32, (tq, B_SLAB), 1) // HEAD_DIM
    for j in range(B_KV_HEADS):
        sl = slice(j * B_SLAB, (j + 1) * B_SLAB)
        qj = q_ref[:, sl]
        zero = jnp.zeros_like(qj)
        qs = jnp.concatenate(
            [jnp.where(lane_grp == g, qj, zero) for g in range(B_GROUP)], axis=0)
        s_c = lax.dot_general(qs, kc_ref[:, sl], NT_DIMS, preferred_element_type=F32)
        s_b = lax.dot_general(qs, kx_ref[pl.ds(kstart, BAND), sl], NT_DIMS,
                              preferred_element_type=F32)
        s_b = jnp.where(band_ok, s_b, NEG_INF)
        sink = jnp.full((rows, 1), sink_ref[j * B_GROUP + B_GROUP - 1], F32)
        for g in range(B_GROUP - 2, -1, -1):
            sink = jnp.where(row1 < (g + 1) * tq, sink_ref[j * B_GROUP + g], sink)
        m = jnp.maximum(jnp.maximum(jnp.max(s_c, axis=-1, keepdims=True),
                                    jnp.max(s_b, axis=-1, keepdims=True)), sink)
        e_c = jnp.exp(s_c - m)
        e_b = jnp.exp(s_b - m)
        l = (jnp.sum(e_c, axis=-1, keepdims=True) + jnp.sum(e_b, axis=-1, keepdims=True)
             + jnp.exp(sink - m))
        o = (jnp.dot(e_c.astype(BF16), vc_ref[:, sl], preferred_element_type=F32)
             + jnp.dot(e_b.astype(BF16), vx_ref[pl.ds(kstart, BAND), sl],
                       preferred_element_type=F32))
        o = o * (1.0 / l)
        o_j = o[(B_GROUP - 1) * tq:]
        for g in range(B_GROUP - 2, -1, -1):
            o_j = jnp.where(lane_grp == g, o[g * tq:(g + 1) * tq], o_j)
        o_ref[:, sl] = (o_j * _silu(gate_ref[:, sl].astype(F32))).astype(BF16)


def _win_attn_call(sink, q, kc, vc, kx, vx, gate):
    bsz, t, _ = q.shape
    n_ctx = kc.shape[1]
    tq = Q_BLOCK
    tile_blk = pl.BlockSpec((None, tq, D_MODEL), lambda b, i: (b, i, 0))
    full_blk = lambda rows: pl.BlockSpec((None, rows, D_MODEL), lambda b, i: (b, 0, 0))
    return pl.pallas_call(
        _win_attn_kernel,
        grid=(bsz, t // tq),
        in_specs=[pl.BlockSpec(memory_space=pltpu.SMEM), tile_blk, full_blk(n_ctx),
                  full_blk(n_ctx), full_blk(t), full_blk(t), tile_blk],
        out_specs=tile_blk,
        out_shape=jax.ShapeDtypeStruct((bsz, t, D_MODEL), BF16),
        compiler_params=_params(2),
        name="win_attn",
    )(sink, q, kc, vc, kx, vx, gate)


def _out_kernel(o_ref, w_ref, x_ref, mod_ref, fg_ref, out_ref, *, final):
    upd = jnp.dot(o_ref[...], w_ref[...], preferred_element_type=F32)
    x = x_ref[...] + mod_ref[:, 2 * D_MODEL:3 * D_MODEL] * upd
    if final:
        ms = jnp.mean(x * x, axis=-1, keepdims=True)
        x = x * lax.rsqrt(ms + NORM_EPS) * fg_ref[...]
    out_ref[...] = x


def _out_call(o, w_o, x, mod, final_g, final, tm, name):
    bsz, t, _ = x.shape
    per_batch_mod = mod.shape[0] != 1
    mod_map = (lambda b, i: (b, 0, 0)) if per_batch_mod else (lambda b, i: (0, 0, 0))
    tile_blk = pl.BlockSpec((None, tm, D_MODEL), lambda b, i: (b, i, 0))
    return pl.pallas_call(
        functools.partial(_out_kernel, final=final),
        grid=(bsz, t // tm),
        in_specs=[tile_blk,
                  pl.BlockSpec((D_MODEL, D_MODEL), lambda b, i: (0, 0)),
                  tile_blk,
                  pl.BlockSpec((None, 1, 3 * D_MODEL), mod_map),
                  pl.BlockSpec((1, D_MODEL), lambda b, i: (0, 0))],
        out_specs=tile_blk,
        out_shape=jax.ShapeDtypeStruct((bsz, t, D_MODEL), F32),
        compiler_params=_params(2),
        name=name,
    )(o, w_o, x, mod, final_g.reshape(1, D_MODEL))


def _rope_tables(seq):
    pos = jnp.arange(seq, dtype=jnp.int32)
    row = (pos // GRID_W).astype(F32)
    col = (pos % GRID_W).astype(F32)
    axis_dim = HEAD_DIM // 2
    inv_freq = ROPE_THETA ** (-jnp.arange(0, axis_dim, 2, dtype=F32) / axis_dim)
    ang_r = row[:, None] * inv_freq
    ang_c = col[:, None] * inv_freq
    zeros = jnp.zeros_like(ang_r)
    reps = LANES // HEAD_DIM
    cos = jnp.tile(jnp.concatenate(
        [jnp.cos(ang_r), jnp.cos(ang_r), jnp.cos(ang_c), jnp.cos(ang_c)], axis=-1), (1, reps))
    sin_a = jnp.tile(jnp.concatenate(
        [-jnp.sin(ang_r), zeros, -jnp.sin(ang_c), zeros], axis=-1), (1, reps))
    sin_b = jnp.tile(jnp.concatenate(
        [zeros, jnp.sin(ang_r), zeros, jnp.sin(ang_c)], axis=-1), (1, reps))
    return cos, sin_a, sin_b


def _replicate_kv_cols(w):
    d = w.shape[0]
    w = w.reshape(d, B_KV_HEADS, 1, HEAD_DIM)
    return jnp.broadcast_to(w, (d, B_KV_HEADS, B_GROUP, HEAD_DIM)).reshape(d, D_MODEL)


def kernel(x, c, ctx, c_ctx, w_mod, b_mod, norm_g, w_o, a_w_in, a_lambda_q1, a_lambda_k1,
           a_lambda_q2, a_lambda_k2, a_subln_g, b_w_in, b_sink, final_g):
    bsz, seq, _ = x.shape
    n_ctx = ctx.shape[1]
    depth = w_mod.shape[0]
    tables = _rope_tables(seq)
    ctx_tables = tuple(tb[:n_ctx] for tb in tables)

    act = jnp.concatenate(
        [c, c_ctx[None, :], jnp.zeros((MOD_ROWS - bsz - 1, D_MODEL), F32)], axis=0)
    mods = _mod_call(act, w_mod, b_mod)

    for i in range(depth):
        last = i == depth - 1
        j = i // 2
        mod_x = mods[i, :bsz].reshape(bsz, 1, 3 * D_MODEL)
        mod_c = mods[i, bsz:bsz + 1].reshape(1, 1, 3 * D_MODEL)
        w_out = w_o[i].astype(BF16)
        if i % 2 == 0:
            w = a_w_in[j].astype(BF16)
            kinds = ("q", "k", "v", "g")
            qx, kx, vx, gx = _proj_call(x, mod_x, norm_g[i], w, tables, kinds, True, 512,
                                        "proj_a_x")
            qc, kc, vc, gc = _proj_call(ctx, mod_c, norm_g[i], w, ctx_tables, kinds, False,
                                        n_ctx, "proj_a_ctx")
            lam_init = 0.8 - 0.6 * math.exp(-0.3 * i)
            lams = (a_lambda_q1[j], a_lambda_k1[j], a_lambda_q2[j], a_lambda_k2[j])
            ox = _diff_attn_call(qx, kc, vc, kx, vx, gx, lams, a_subln_g[j], lam_init, 256,
                                 "diff_attn_x")
            oc = None
            if not last:
                oc = _diff_attn_call(qc, kc, vc, None, None, gc, lams, a_subln_g[j],
                                     lam_init, n_ctx, "diff_attn_ctx")
        else:
            wb = b_w_in[j]
            wq = wb[:, :D_MODEL]
            wk = _replicate_kv_cols(wb[:, D_MODEL:D_MODEL + B_KV_HEADS * HEAD_DIM])
            wv = _replicate_kv_cols(
                wb[:, D_MODEL + B_KV_HEADS * HEAD_DIM:D_MODEL + 2 * B_KV_HEADS * HEAD_DIM])
            wg = wb[:, D_MODEL + 2 * B_KV_HEADS * HEAD_DIM:]
            w = jnp.concatenate([wq, wk, wv, wg], axis=1).astype(BF16)
            qx, kx, vx, gx = _proj_call(x, mod_x, norm_g[i], w, tables,
                                        ("q", "k", "v", "g"), True, 512, "proj_b_x")
            if last:
                w_c = jnp.concatenate([wk, wv], axis=1).astype(BF16)
                kc, vc = _proj_call(ctx, mod_c, norm_g[i], w_c, ctx_tables, ("k", "v"),
                                    False, n_ctx, "proj_b_ctx")
                ox = _win_attn_call(b_sink[j], qx, kc, vc, kx, vx, gx)
                oc = None
            else:
                raise NotImplementedError("context output of a windowed layer")
        x = _out_call(ox, w_out, x, mod_x, final_g, last, 512, "out_x")
        if not last:
            ctx = _out_call(oc, w_out, ctx, mod_c, final_g, False, n_ctx, "out_ctx")
    return x
```

```python
import functools
import math

import jax
import jax.numpy as jnp
from jax import lax
from jax.experimental import pallas as pl
from jax.experimental.pallas import tpu as pltpu

D_MODEL = 1024
GRID_W = 64
HEAD_DIM = 64
LANES = 128
A_HEADS = D_MODEL // (2 * HEAD_DIM)
B_HEADS = D_MODEL // HEAD_DIM
B_KV_HEADS = 4
B_GROUP = B_HEADS // B_KV_HEADS
B_SLAB = B_GROUP * HEAD_DIM
WINDOW = 128
Q_BLOCK = 128
BAND = Q_BLOCK + 2 * WINDOW
ROPE_THETA = 10000.0
NORM_EPS = 1e-6
SUBLN_EPS = 1e-5
NEG_INF = -1e30
ATTN_SCALE = HEAD_DIM ** -0.5
LOG2E = math.log2(math.e)
MOD_ROWS = 40
VMEM_LIMIT = 56 * 1024 * 1024

F32 = jnp.float32
BF16 = jnp.bfloat16
NT_DIMS = (((1,), (1,)), ((), ()))


def _params(n_grid, flags=None):
    return pltpu.CompilerParams(
        dimension_semantics=("arbitrary",) * n_grid,
        vmem_limit_bytes=VMEM_LIMIT, flags=flags)


def _silu(x):
    return x * (1.0 / (1.0 + jnp.exp(-x)))


def _mod_kernel(act_ref, w_ref, b_ref, out_ref):
    a = _silu(act_ref[...])
    out_ref[...] = jnp.dot(a, w_ref[...], preferred_element_type=F32,
                           precision=lax.Precision.HIGHEST) + b_ref[...]


def _mod_call(act, w_mod, b_mod):
    depth = w_mod.shape[0]
    n_blk = w_mod.shape[2] // D_MODEL
    return pl.pallas_call(
        _mod_kernel,
        grid=(depth, n_blk),
        in_specs=[
            pl.BlockSpec((MOD_ROWS, D_MODEL), lambda l, n: (0, 0)),
            pl.BlockSpec((None, D_MODEL, D_MODEL), lambda l, n: (l, 0, n)),
            pl.BlockSpec((None, 1, D_MODEL), lambda l, n: (l, 0, n)),
        ],
        out_specs=pl.BlockSpec((None, MOD_ROWS, D_MODEL), lambda l, n: (l, 0, n)),
        out_shape=jax.ShapeDtypeStruct((depth, MOD_ROWS, 3 * D_MODEL), F32),
        compiler_params=_params(2),
        name="mod_vectors",
    )(act, w_mod, b_mod.reshape(depth, 1, 3 * D_MODEL))


def _rope_block(r, cos, sin_a, sin_b):
    return (r * cos + pltpu.roll(r, LANES - 16, 1) * sin_a
            + pltpu.roll(r, 16, 1) * sin_b)


def _proj_kernel(x_ref, mod_ref, g_ref, w_ref, rep_ref, cos_ref, sa_ref, sb_ref, *out_refs,
                 groups, rope):
    x = x_ref[...]
    ms = jnp.mean(x * x, axis=-1, keepdims=True)
    y = x * lax.rsqrt(ms + NORM_EPS) * g_ref[...]
    shift = mod_ref[:, 0:D_MODEL]
    scale = mod_ref[:, D_MODEL:2 * D_MODEL]
    h = (y * (1.0 + scale) + shift).astype(BF16)

    def rope_cols(r):
        if not rope:
            return r
        cos, sin_a, sin_b = cos_ref[...], sa_ref[...], sb_ref[...]
        return jnp.concatenate(
            [_rope_block(r[:, hb * LANES:(hb + 1) * LANES], cos, sin_a, sin_b)
             for hb in range(r.shape[1] // LANES)], axis=1)

    outs = list(out_refs)
    for kind, c0 in groups:
        if kind == "kv":
            kv_w = rep_ref.shape[0]
            r = jnp.dot(h, w_ref[:, c0:c0 + 2 * kv_w], preferred_element_type=F32)
            for part in (rope_cols(r[:, :kv_w]), r[:, kv_w:]):
                outs.pop(0)[...] = jnp.dot(part.astype(BF16), rep_ref[...],
                                           preferred_element_type=F32).astype(BF16)
            continue
        r = jnp.dot(h, w_ref[:, c0:c0 + D_MODEL], preferred_element_type=F32)
        if kind == "q":
            r = r * (ATTN_SCALE * LOG2E)
        if kind in ("q", "k"):
            r = rope_cols(r)
        outs.pop(0)[...] = r.astype(BF16)


def _proj_call(x, mod, norm_g, w, rep, tables, groups, rope, tm, name):
    bsz, t, _ = x.shape
    per_batch_mod = mod.shape[0] != 1
    mod_map = (lambda b, i: (b, 0, 0)) if per_batch_mod else (lambda b, i: (0, 0, 0))
    n_out = sum(2 if kind == "kv" else 1 for kind, _ in groups)
    return pl.pallas_call(
        functools.partial(_proj_kernel, groups=groups, rope=rope),
        grid=(bsz, t // tm),
        in_specs=[
            pl.BlockSpec((None, tm, D_MODEL), lambda b, i: (b, i, 0)),
            pl.BlockSpec((None, 1, 3 * D_MODEL), mod_map),
            pl.BlockSpec((1, D_MODEL), lambda b, i: (0, 0)),
            pl.BlockSpec(w.shape, lambda b, i: (0, 0)),
            pl.BlockSpec(rep.shape, lambda b, i: (0, 0)),
            pl.BlockSpec((tm, LANES), lambda b, i: (i, 0)),
            pl.BlockSpec((tm, LANES), lambda b, i: (i, 0)),
            pl.BlockSpec((tm, LANES), lambda b, i: (i, 0)),
        ],
        out_specs=[pl.BlockSpec((None, tm, D_MODEL), lambda b, i: (b, i, 0))] * n_out,
        out_shape=[jax.ShapeDtypeStruct((bsz, t, D_MODEL), BF16)] * n_out,
        compiler_params=_params(2),
        name=name,
    )(x, mod, norm_g.reshape(1, D_MODEL), w, rep, *tables)


def _diff_scores(q_ref, kc_ref, kx_ref, h):
    sl = slice(h * LANES, (h + 1) * LANES)
    q = q_ref[:, sl]
    lane = lax.broadcasted_iota(jnp.int32, q.shape, 1)
    zero = jnp.zeros_like(q)
    qs = jnp.concatenate([jnp.where(lane < HEAD_DIM, q, zero),
                          jnp.where(lane >= HEAD_DIM, q, zero)], axis=0)
    s_c = lax.dot_general(qs, kc_ref[:, sl], NT_DIMS, preferred_element_type=F32)
    s_x = None
    if kx_ref is not None:
        s_x = lax.dot_general(qs, kx_ref[:, sl], NT_DIMS, preferred_element_type=F32)
    return s_c, s_x


def _diff_attn_kernel(*refs, with_latent, lam_init):
    if with_latent:
        (q_ref, kc_ref, vc_ref, kx_ref, vx_ref, gate_ref, lq1_ref, lk1_ref, lq2_ref,
         lk2_ref, sg_ref, o_ref) = refs
    else:
        (q_ref, kc_ref, vc_ref, gate_ref, lq1_ref, lk1_ref, lq2_ref, lk2_ref, sg_ref,
         o_ref) = refs
        kx_ref = vx_ref = None
    tq = q_ref.shape[0]
    lam = (jnp.exp(jnp.sum(lq1_ref[...] * lk1_ref[...], axis=-1, keepdims=True))
           - jnp.exp(jnp.sum(lq2_ref[...] * lk2_ref[...], axis=-1, keepdims=True))
           + lam_init)
    scores = _diff_scores(q_ref, kc_ref, kx_ref, 0)
    for h in range(A_HEADS):
        sl = slice(h * LANES, (h + 1) * LANES)
        s_c, s_x = scores
        if h + 1 < A_HEADS:
            scores = _diff_scores(q_ref, kc_ref, kx_ref, h + 1)
        m = jnp.max(s_c, axis=-1, keepdims=True)
        if with_latent:
            m = jnp.maximum(m, jnp.max(s_x, axis=-1, keepdims=True))
        p_c = jnp.exp2(s_c - m)
        l = jnp.sum(p_c, axis=-1, keepdims=True)
        if with_latent:
            p_x = jnp.exp2(s_x - m)
            l = l + jnp.sum(p_x, axis=-1, keepdims=True)
        inv_l1 = 1.0 / l[:tq]
        r = lam * l[:tq] * (1.0 / l[tq:])
        a_c = (p_c[:tq] - r * p_c[tq:]).astype(BF16)
        o = jnp.dot(a_c, vc_ref[:, sl], preferred_element_type=F32)
        if with_latent:
            a_x = (p_x[:tq] - r * p_x[tq:]).astype(BF16)
            o = o + jnp.dot(a_x, vx_ref[:, sl], preferred_element_type=F32)
        o = o * inv_l1
        ms = jnp.mean(o * o, axis=-1, keepdims=True)
        o = o * lax.rsqrt(ms + SUBLN_EPS) * sg_ref[...] * (1.0 - lam_init)
        o_ref[:, sl] = (o * _silu(gate_ref[:, sl].astype(F32))).astype(BF16)


def _diff_attn_call(q, kc, vc, kx, vx, gate, lams, subln_g, lam_init, tq, name):
    bsz, t, _ = q.shape
    n_ctx = kc.shape[1]
    with_latent = kx is not None
    full_blk = lambda rows: pl.BlockSpec((None, rows, D_MODEL), lambda b, i: (b, 0, 0))
    tile_blk = pl.BlockSpec((None, tq, D_MODEL), lambda b, i: (b, i, 0))
    vec_blk = lambda n: pl.BlockSpec((1, n), lambda b, i: (0, 0))
    in_specs = [tile_blk, full_blk(n_ctx), full_blk(n_ctx)]
    args = [q, kc, vc]
    if with_latent:
        in_specs += [full_blk(kx.shape[1]), full_blk(kx.shape[1])]
        args += [kx, vx]
    in_specs += [tile_blk] + [vec_blk(HEAD_DIM)] * 4 + [vec_blk(2 * HEAD_DIM)]
    args += [gate] + [v.reshape(1, HEAD_DIM) for v in lams] + [subln_g.reshape(1, -1)]
    return pl.pallas_call(
        functools.partial(_diff_attn_kernel, with_latent=with_latent, lam_init=lam_init),
        grid=(bsz, t // tq),
        in_specs=in_specs,
        out_specs=tile_blk,
        out_shape=jax.ShapeDtypeStruct((bsz, t, D_MODEL), BF16),
        compiler_params=_params(2),
        name=name,
    )(*args)


def _win_attn_kernel(sink_ref, q_ref, kc_ref, vc_ref, kx_ref, vx_ref, gate_ref, bias_ref,
                     o_ref):
    i = pl.program_id(1)
    tq = q_ref.shape[0]
    seq = kx_ref.shape[0]
    kstart = pl.multiple_of(jnp.clip(i * tq - WINDOW, 0, seq - BAND), WINDOW)
    rows = B_GROUP * tq
    row1 = lax.broadcasted_iota(jnp.int32, (rows, 1), 0)
    lane_grp = lax.broadcasted_iota(jnp.int32, (tq, B_SLAB), 1) // HEAD_DIM
    for j in range(B_KV_HEADS):
        sl = slice(j * B_SLAB, (j + 1) * B_SLAB)
        qj = q_ref[:, sl]
        zero = jnp.zeros_like(qj)
        qs = jnp.concatenate(
            [jnp.where(lane_grp == g, qj, zero) for g in range(B_GROUP)], axis=0)
        s_c = lax.dot_general(qs, kc_ref[:, sl], NT_DIMS, preferred_element_type=F32)
        s_b = lax.dot_general(qs, kx_ref[pl.ds(kstart, BAND), sl], NT_DIMS,
                              preferred_element_type=F32)
        s_b = s_b + bias_ref[...]
        sink = jnp.full((rows, 1), sink_ref[j * B_GROUP + B_GROUP - 1], F32)
        for g in range(B_GROUP - 2, -1, -1):
            sink = jnp.where(row1 < (g + 1) * tq, sink_ref[j * B_GROUP + g], sink)
        sink = sink * LOG2E
        m = jnp.maximum(jnp.maximum(jnp.max(s_c, axis=-1, keepdims=True),
                                    jnp.max(s_b, axis=-1, keepdims=True)), sink)
        e_c = jnp.exp2(s_c - m)
        e_b = jnp.exp2(s_b - m)
        l = (jnp.sum(e_c, axis=-1, keepdims=True) + jnp.sum(e_b, axis=-1, keepdims=True)
             + jnp.exp2(sink - m))
        o = (jnp.dot(e_c.astype(BF16), vc_ref[:, sl], preferred_element_type=F32)
             + jnp.dot(e_b.astype(BF16), vx_ref[pl.ds(kstart, BAND), sl],
                       preferred_element_type=F32))
        o = o * (1.0 / l)
        o_j = o[(B_GROUP - 1) * tq:]
        for g in range(B_GROUP - 2, -1, -1):
            o_j = jnp.where(lane_grp == g, o[g * tq:(g + 1) * tq], o_j)
        o_ref[:, sl] = (o_j * _silu(gate_ref[:, sl].astype(F32))).astype(BF16)


def _band_bias(tq):
    r = jnp.arange(B_GROUP * tq, dtype=jnp.int32)[None, :, None] % tq
    c = jnp.arange(BAND, dtype=jnp.int32)[None, None, :]
    off = (jnp.arange(3, dtype=jnp.int32) * WINDOW)[:, None, None]
    return jnp.where(jnp.abs(c - off - r) <= WINDOW, 0.0, NEG_INF).astype(F32)


def _win_attn_call(sink, q, kc, vc, kx, vx, gate):
    bsz, t, _ = q.shape
    n_ctx = kc.shape[1]
    tq = Q_BLOCK
    n_tiles = t // tq
    assert n_tiles >= 3 and tq == WINDOW
    tile_blk = pl.BlockSpec((None, tq, D_MODEL), lambda b, i: (b, i, 0))
    full_blk = lambda rows: pl.BlockSpec((None, rows, D_MODEL), lambda b, i: (b, 0, 0))
    bias_blk = pl.BlockSpec(
        (None, B_GROUP * tq, BAND),
        lambda b, i: (jnp.where(i == 0, 0, jnp.where(i == n_tiles - 1, 2, 1)), 0, 0))
    return pl.pallas_call(
        _win_attn_kernel,
        grid=(bsz, n_tiles),
        in_specs=[pl.BlockSpec(memory_space=pltpu.SMEM), tile_blk, full_blk(n_ctx),
                  full_blk(n_ctx), full_blk(t), full_blk(t), tile_blk, bias_blk],
        out_specs=tile_blk,
        out_shape=jax.ShapeDtypeStruct((bsz, t, D_MODEL), BF16),
        compiler_params=_params(2),
        name="win_attn",
    )(sink, q, kc, vc, kx, vx, gate, _band_bias(tq))


def _out_kernel(o_ref, w_ref, x_ref, mod_ref, fg_ref, out_ref, *, final):
    upd = jnp.dot(o_ref[...], w_ref[...], preferred_element_type=F32)
    x = x_ref[...] + mod_ref[:, 2 * D_MODEL:3 * D_MODEL] * upd
    if final:
        ms = jnp.mean(x * x, axis=-1, keepdims=True)
        x = x * lax.rsqrt(ms + NORM_EPS) * fg_ref[...]
    out_ref[...] = x


def _out_call(o, w_o, x, mod, final_g, final, tm, name):
    bsz, t, _ = x.shape
    per_batch_mod = mod.shape[0] != 1
    mod_map = (lambda b, i: (b, 0, 0)) if per_batch_mod else (lambda b, i: (0, 0, 0))
    tile_blk = pl.BlockSpec((None, tm, D_MODEL), lambda b, i: (b, i, 0))
    return pl.pallas_call(
        functools.partial(_out_kernel, final=final),
        grid=(bsz, t // tm),
        in_specs=[tile_blk,
                  pl.BlockSpec((D_MODEL, D_MODEL), lambda b, i: (0, 0)),
                  tile_blk,
                  pl.BlockSpec((None, 1, 3 * D_MODEL), mod_map),
                  pl.BlockSpec((1, D_MODEL), lambda b, i: (0, 0))],
        out_specs=tile_blk,
        out_shape=jax.ShapeDtypeStruct((bsz, t, D_MODEL), F32),
        compiler_params=_params(2),
        name=name,
    )(o, w_o, x, mod, final_g.reshape(1, D_MODEL))


def _rope_tables(seq):
    pos = jnp.arange(seq, dtype=jnp.int32)
    row = (pos // GRID_W).astype(F32)
    col = (pos % GRID_W).astype(F32)
    axis_dim = HEAD_DIM // 2
    inv_freq = ROPE_THETA ** (-jnp.arange(0, axis_dim, 2, dtype=F32) / axis_dim)
    ang_r = row[:, None] * inv_freq
    ang_c = col[:, None] * inv_freq
    zeros = jnp.zeros_like(ang_r)
    reps = LANES // HEAD_DIM
    cos = jnp.tile(jnp.concatenate(
        [jnp.cos(ang_r), jnp.cos(ang_r), jnp.cos(ang_c), jnp.cos(ang_c)], axis=-1), (1, reps))
    sin_a = jnp.tile(jnp.concatenate(
        [-jnp.sin(ang_r), zeros, -jnp.sin(ang_c), zeros], axis=-1), (1, reps))
    sin_b = jnp.tile(jnp.concatenate(
        [zeros, jnp.sin(ang_r), zeros, jnp.sin(ang_c)], axis=-1), (1, reps))
    return cos, sin_a, sin_b


def _kv_replication_matrix():
    col = jnp.arange(D_MODEL, dtype=jnp.int32)
    src = (col // B_SLAB) * HEAD_DIM + col % HEAD_DIM
    row = jnp.arange(B_KV_HEADS * HEAD_DIM, dtype=jnp.int32)
    return (row[:, None] == src[None, :]).astype(BF16)


def kernel(x, c, ctx, c_ctx, w_mod, b_mod, norm_g, w_o, a_w_in, a_lambda_q1, a_lambda_k1,
           a_lambda_q2, a_lambda_k2, a_subln_g, b_w_in, b_sink, final_g):
    bsz, seq, _ = x.shape
    n_ctx = ctx.shape[1]
    depth = w_mod.shape[0]
    tables = _rope_tables(seq)
    ctx_tables = tuple(tb[:n_ctx] for tb in tables)

    act = jnp.concatenate(
        [c, c_ctx[None, :], jnp.zeros((MOD_ROWS - bsz - 1, D_MODEL), F32)], axis=0)
    mods = _mod_call(act, w_mod, b_mod)

    rep = _kv_replication_matrix()
    kv_w = B_KV_HEADS * HEAD_DIM
    for i in range(depth):
        last = i == depth - 1
        j = i // 2
        mod_x = mods[i, :bsz].reshape(bsz, 1, 3 * D_MODEL)
        mod_c = mods[i, bsz:bsz + 1].reshape(1, 1, 3 * D_MODEL)
        w_out = w_o[i].astype(BF16)
        if i % 2 == 0:
            w = a_w_in[j].astype(BF16)
            groups = tuple((kind, n * D_MODEL) for n, kind in enumerate("qkvg"))
            qx, kx, vx, gx = _proj_call(x, mod_x, norm_g[i], w, rep, tables, groups, True,
                                        512, "proj_a_x")
            qc, kc, vc, gc = _proj_call(ctx, mod_c, norm_g[i], w, rep, ctx_tables, groups,
                                        False, n_ctx, "proj_a_ctx")
            lam_init = 0.8 - 0.6 * math.exp(-0.3 * i)
            lams = (a_lambda_q1[j], a_lambda_k1[j], a_lambda_q2[j], a_lambda_k2[j])
            ox = _diff_attn_call(qx, kc, vc, kx, vx, gx, lams, a_subln_g[j], lam_init, 256,
                                 "diff_attn_x")
            oc = None
            if not last:
                oc = _diff_attn_call(qc, kc, vc, None, None, gc, lams, a_subln_g[j],
                                     lam_init, n_ctx, "diff_attn_ctx")
        else:
            w = b_w_in[j].astype(BF16)
            groups = (("q", 0), ("kv", D_MODEL), ("g", D_MODEL + 2 * kv_w))
            qx, kx, vx, gx = _proj_call(x, mod_x, norm_g[i], w, rep, tables, groups, True,
                                        512, "proj_b_x")
            if last:
                kc, vc = _proj_call(ctx, mod_c, norm_g[i], w, rep, ctx_tables,
                                    (("kv", D_MODEL),), False, n_ctx, "proj_b_ctx")
                ox = _win_attn_call(b_sink[j], qx, kc, vc, kx, vx, gx)
                oc = None
            else:
                raise NotImplementedError("context output of a windowed layer")
        x = _out_call(ox, w_out, x, mod_x, final_g, last, 512, "out_x")
        if not last:
            ctx = _out_call(oc, w_out, ctx, mod_c, final_g, False, n_ctx, "out_ctx")
    return x
```

```python
import functools
import math

import jax
import jax.numpy as jnp
from jax import lax
from jax.experimental import pallas as pl
from jax.experimental.pallas import tpu as pltpu

D_MODEL = 1024
GRID_W = 64
HEAD_DIM = 64
LANES = 128
A_HEADS = D_MODEL // (2 * HEAD_DIM)
B_HEADS = D_MODEL // HEAD_DIM
B_KV_HEADS = 4
B_GROUP = B_HEADS // B_KV_HEADS
B_SLAB = B_GROUP * HEAD_DIM
WINDOW = 128
Q_BLOCK = 128
BAND = Q_BLOCK + 2 * WINDOW
WIN_TILES_PER_STEP = 4
ROPE_THETA = 10000.0
NORM_EPS = 1e-6
SUBLN_EPS = 1e-5
NEG_INF = -1e30
ATTN_SCALE = HEAD_DIM ** -0.5
LOG2E = math.log2(math.e)
MOD_ROWS = 40
VMEM_LIMIT = 56 * 1024 * 1024

F32 = jnp.float32
BF16 = jnp.bfloat16
NT_DIMS = (((1,), (1,)), ((), ()))


def _params(n_grid, flags=None):
    return pltpu.CompilerParams(
        dimension_semantics=("arbitrary",) * n_grid,
        vmem_limit_bytes=VMEM_LIMIT, flags=flags)


def _silu(x):
    return x * (1.0 / (1.0 + jnp.exp(-x)))


def _mod_kernel(act_ref, w_ref, b_ref, out_ref):
    a = _silu(act_ref[...])
    out_ref[...] = jnp.dot(a, w_ref[...], preferred_element_type=F32,
                           precision=lax.Precision.HIGHEST) + b_ref[...]


def _mod_call(act, w_mod, b_mod):
    depth = w_mod.shape[0]
    n_blk = w_mod.shape[2] // D_MODEL
    return pl.pallas_call(
        _mod_kernel,
        grid=(depth, n_blk),
        in_specs=[
            pl.BlockSpec((MOD_ROWS, D_MODEL), lambda l, n: (0, 0)),
            pl.BlockSpec((None, D_MODEL, D_MODEL), lambda l, n: (l, 0, n)),
            pl.BlockSpec((None, 1, D_MODEL), lambda l, n: (l, 0, n)),
        ],
        out_specs=pl.BlockSpec((None, MOD_ROWS, D_MODEL), lambda l, n: (l, 0, n)),
        out_shape=jax.ShapeDtypeStruct((depth, MOD_ROWS, 3 * D_MODEL), F32),
        compiler_params=_params(2),
        name="mod_vectors",
    )(act, w_mod, b_mod.reshape(depth, 1, 3 * D_MODEL))


def _rope_block(r, cos, sin_a, sin_b):
    return (r * cos + pltpu.roll(r, LANES - 16, 1) * sin_a
            + pltpu.roll(r, 16, 1) * sin_b)


def _proj_kernel(x_ref, mod_ref, g_ref, w_ref, rep_ref, cos_ref, sa_ref, sb_ref, *out_refs,
                 groups, rope):
    x = x_ref[...]
    ms = jnp.mean(x * x, axis=-1, keepdims=True)
    y = x * lax.rsqrt(ms + NORM_EPS) * g_ref[...]
    shift = mod_ref[:, 0:D_MODEL]
    scale = mod_ref[:, D_MODEL:2 * D_MODEL]
    h = (y * (1.0 + scale) + shift).astype(BF16)

    def rope_cols(r):
        if not rope:
            return r
        cos, sin_a, sin_b = cos_ref[...], sa_ref[...], sb_ref[...]
        return jnp.concatenate(
            [_rope_block(r[:, hb * LANES:(hb + 1) * LANES], cos, sin_a, sin_b)
             for hb in range(r.shape[1] // LANES)], axis=1)

    outs = list(out_refs)
    for kind, c0 in groups:
        if kind == "kv":
            kv_w = rep_ref.shape[0]
            r = jnp.dot(h, w_ref[:, c0:c0 + 2 * kv_w], preferred_element_type=F32)
            for part in (rope_cols(r[:, :kv_w]), r[:, kv_w:]):
                outs.pop(0)[...] = jnp.dot(part.astype(BF16), rep_ref[...],
                                           preferred_element_type=F32).astype(BF16)
            continue
        r = jnp.dot(h, w_ref[:, c0:c0 + D_MODEL], preferred_element_type=F32)
        if kind == "q":
            r = r * (ATTN_SCALE * LOG2E)
        if kind in ("q", "k"):
            r = rope_cols(r)
        outs.pop(0)[...] = r.astype(BF16)


def _proj_call(x, mod, norm_g, w, rep, tables, groups, rope, tm, name):
    bsz, t, _ = x.shape
    per_batch_mod = mod.shape[0] != 1
    mod_map = (lambda b, i: (b, 0, 0)) if per_batch_mod else (lambda b, i: (0, 0, 0))
    n_out = sum(2 if kind == "kv" else 1 for kind, _ in groups)
    return pl.pallas_call(
        functools.partial(_proj_kernel, groups=groups, rope=rope),
        grid=(bsz, t // tm),
        in_specs=[
            pl.BlockSpec((None, tm, D_MODEL), lambda b, i: (b, i, 0)),
            pl.BlockSpec((None, 1, 3 * D_MODEL), mod_map),
            pl.BlockSpec((1, D_MODEL), lambda b, i: (0, 0)),
            pl.BlockSpec(w.shape, lambda b, i: (0, 0)),
            pl.BlockSpec(rep.shape, lambda b, i: (0, 0)),
            pl.BlockSpec((tm, LANES), lambda b, i: (i, 0)),
            pl.BlockSpec((tm, LANES), lambda b, i: (i, 0)),
            pl.BlockSpec((tm, LANES), lambda b, i: (i, 0)),
        ],
        out_specs=[pl.BlockSpec((None, tm, D_MODEL), lambda b, i: (b, i, 0))] * n_out,
        out_shape=[jax.ShapeDtypeStruct((bsz, t, D_MODEL), BF16)] * n_out,
        compiler_params=_params(2),
        name=name,
    )(x, mod, norm_g.reshape(1, D_MODEL), w, rep, *tables)


def _diff_scores(q_ref, kc_ref, kx_ref, h):
    sl = slice(h * LANES, (h + 1) * LANES)
    q = q_ref[:, sl]
    lane = lax.broadcasted_iota(jnp.int32, q.shape, 1)
    zero = jnp.zeros_like(q)
    qs = jnp.concatenate([jnp.where(lane < HEAD_DIM, q, zero),
                          jnp.where(lane >= HEAD_DIM, q, zero)], axis=0)
    s_c = lax.dot_general(qs, kc_ref[:, sl], NT_DIMS, preferred_element_type=F32)
    s_x = None
    if kx_ref is not None:
        s_x = lax.dot_general(qs, kx_ref[:, sl], NT_DIMS, preferred_element_type=F32)
    return s_c, s_x


def _out_proj_residual(o_scr, w_ref, x_ref, mod_ref, fg_ref, out_ref, final):
    upd = jnp.dot(o_scr[...], w_ref[...], preferred_element_type=F32)
    x = x_ref[...] + mod_ref[:, 2 * D_MODEL:3 * D_MODEL] * upd
    if final:
        ms = jnp.mean(x * x, axis=-1, keepdims=True)
        x = x * lax.rsqrt(ms + NORM_EPS) * fg_ref[...]
    out_ref[...] = x


def _diff_attn_kernel(*refs, with_latent, lam_init, final):
    if with_latent:
        (q_ref, kc_ref, vc_ref, kx_ref, vx_ref, gate_ref, lq1_ref, lk1_ref, lq2_ref,
         lk2_ref, sg_ref, w_ref, x_ref, mod_ref, fg_ref, out_ref, o_scr) = refs
    else:
        (q_ref, kc_ref, vc_ref, gate_ref, lq1_ref, lk1_ref, lq2_ref, lk2_ref, sg_ref,
         w_ref, x_ref, mod_ref, fg_ref, out_ref, o_scr) = refs
        kx_ref = vx_ref = None
    tq = q_ref.shape[0]
    lam = (jnp.exp(jnp.sum(lq1_ref[...] * lk1_ref[...], axis=-1, keepdims=True))
           - jnp.exp(jnp.sum(lq2_ref[...] * lk2_ref[...], axis=-1, keepdims=True))
           + lam_init)
    scores = _diff_scores(q_ref, kc_ref, kx_ref, 0)
    for h in range(A_HEADS):
        sl = slice(h * LANES, (h + 1) * LANES)
        s_c, s_x = scores
        if h + 1 < A_HEADS:
            scores = _diff_scores(q_ref, kc_ref, kx_ref, h + 1)
        m = jnp.max(s_c, axis=-1, keepdims=True)
        if with_latent:
            m = jnp.maximum(m, jnp.max(s_x, axis=-1, keepdims=True))
        ones = jnp.ones((vc_ref.shape[0], LANES), BF16)
        p_c = jnp.exp2(s_c - m).astype(BF16)
        oa = jnp.dot(p_c, jnp.concatenate([vc_ref[:, sl], ones], axis=1),
                     preferred_element_type=F32)
        if with_latent:
            ones = jnp.ones((vx_ref.shape[0], LANES), BF16)
            p_x = jnp.exp2(s_x - m).astype(BF16)
            oa = oa + jnp.dot(p_x, jnp.concatenate([vx_ref[:, sl], ones], axis=1),
                              preferred_element_type=F32)
        on = oa[:, :LANES] * (1.0 / oa[:, LANES:])
        o = on[:tq] - lam * on[tq:]
        ms = jnp.mean(o * o, axis=-1, keepdims=True)
        o = o * lax.rsqrt(ms + SUBLN_EPS) * sg_ref[...] * (1.0 - lam_init)
        o_scr[:, sl] = (o * _silu(gate_ref[:, sl].astype(F32))).astype(BF16)
    _out_proj_residual(o_scr, w_ref, x_ref, mod_ref, fg_ref, out_ref, final)


def _mod_spec(mod):
    if mod.shape[0] != 1:
        return pl.BlockSpec((None, 1, 3 * D_MODEL), lambda b, i: (b, 0, 0))
    return pl.BlockSpec((None, 1, 3 * D_MODEL), lambda b, i: (0, 0, 0))


def _diff_attn_call(q, kc, vc, kx, vx, gate, lams, subln_g, lam_init, w_o, x, mod, final_g,
                    final, tq, name):
    bsz, t, _ = q.shape
    n_ctx = kc.shape[1]
    with_latent = kx is not None
    full_blk = lambda rows: pl.BlockSpec((None, rows, D_MODEL), lambda b, i: (b, 0, 0))
    tile_blk = pl.BlockSpec((None, tq, D_MODEL), lambda b, i: (b, i, 0))
    const_blk = lambda r, n: pl.BlockSpec((r, n), lambda b, i: (0, 0))
    in_specs = [tile_blk, full_blk(n_ctx), full_blk(n_ctx)]
    args = [q, kc, vc]
    if with_latent:
        in_specs += [full_blk(kx.shape[1]), full_blk(kx.shape[1])]
        args += [kx, vx]
    in_specs += ([tile_blk] + [const_blk(1, HEAD_DIM)] * 4 + [const_blk(1, 2 * HEAD_DIM)]
                 + [const_blk(D_MODEL, D_MODEL), tile_blk, _mod_spec(mod),
                    const_blk(1, D_MODEL)])
    args += ([gate] + [v.reshape(1, HEAD_DIM) for v in lams] + [subln_g.reshape(1, -1)]
             + [w_o, x, mod, final_g.reshape(1, D_MODEL)])
    return pl.pallas_call(
        functools.partial(_diff_attn_kernel, with_latent=with_latent, lam_init=lam_init,
                          final=final),
        grid=(bsz, t // tq),
        in_specs=in_specs,
        out_specs=tile_blk,
        out_shape=jax.ShapeDtypeStruct((bsz, t, D_MODEL), F32),
        scratch_shapes=[pltpu.VMEM((tq, D_MODEL), BF16)],
        compiler_params=_params(2),
        name=name,
    )(*args)


def _win_attn_kernel(sink_ref, q_ref, kc_ref, vc_ref, kx_ref, vx_ref, gate_ref, bias_ref,
                     w_ref, x_ref, mod_ref, fg_ref, out_ref, o_scr, *, final):
    tq = Q_BLOCK
    tiles = q_ref.shape[0] // tq
    seq = kx_ref.shape[0]
    n_tiles = seq // tq
    rows = B_GROUP * tq
    row1 = lax.broadcasted_iota(jnp.int32, (rows, 1), 0)
    lane_grp = lax.broadcasted_iota(jnp.int32, (tq, B_SLAB), 1) // HEAD_DIM
    for t in range(tiles):
        ti = pl.program_id(1) * tiles + t
        kstart = pl.multiple_of(jnp.clip(ti * tq - WINDOW, 0, seq - BAND), WINDOW)
        bias = bias_ref[jnp.where(ti == 0, 0, jnp.where(ti == n_tiles - 1, 2, 1))]
        rs = slice(t * tq, (t + 1) * tq)
        for j in range(B_KV_HEADS):
            sl = slice(j * B_SLAB, (j + 1) * B_SLAB)
            qj = q_ref[rs, sl]
            zero = jnp.zeros_like(qj)
            qs = jnp.concatenate(
                [jnp.where(lane_grp == g, qj, zero) for g in range(B_GROUP)], axis=0)
            s_c = lax.dot_general(qs, kc_ref[:, sl], NT_DIMS, preferred_element_type=F32)
            s_b = lax.dot_general(qs, kx_ref[pl.ds(kstart, BAND), sl], NT_DIMS,
                                  preferred_element_type=F32)
            s_b = s_b + bias
            sink = jnp.full((rows, 1), sink_ref[j * B_GROUP + B_GROUP - 1], F32)
            for g in range(B_GROUP - 2, -1, -1):
                sink = jnp.where(row1 < (g + 1) * tq, sink_ref[j * B_GROUP + g], sink)
            sink = sink * LOG2E
            m = jnp.maximum(jnp.maximum(jnp.max(s_c, axis=-1, keepdims=True),
                                        jnp.max(s_b, axis=-1, keepdims=True)), sink)
            e_c = jnp.exp2(s_c - m)
            e_b = jnp.exp2(s_b - m)
            l = (jnp.sum(e_c, axis=-1, keepdims=True)
                 + jnp.sum(e_b, axis=-1, keepdims=True) + jnp.exp2(sink - m))
            o = (jnp.dot(e_c.astype(BF16), vc_ref[:, sl], preferred_element_type=F32)
                 + jnp.dot(e_b.astype(BF16), vx_ref[pl.ds(kstart, BAND), sl],
                           preferred_element_type=F32))
            o = o * (1.0 / l)
            o_j = o[(B_GROUP - 1) * tq:]
            for g in range(B_GROUP - 2, -1, -1):
                o_j = jnp.where(lane_grp == g, o[g * tq:(g + 1) * tq], o_j)
            o_scr[rs, sl] = (o_j * _silu(gate_ref[rs, sl].astype(F32))).astype(BF16)
    _out_proj_residual(o_scr, w_ref, x_ref, mod_ref, fg_ref, out_ref, final)


def _band_bias(tq):
    r = jnp.arange(B_GROUP * tq, dtype=jnp.int32)[None, :, None] % tq
    c = jnp.arange(BAND, dtype=jnp.int32)[None, None, :]
    off = (jnp.arange(3, dtype=jnp.int32) * WINDOW)[:, None, None]
    return jnp.where(jnp.abs(c - off - r) <= WINDOW, 0.0, NEG_INF).astype(F32)


def _win_attn_call(sink, q, kc, vc, kx, vx, gate, w_o, x, mod, final_g, final):
    bsz, t, _ = q.shape
    n_ctx = kc.shape[1]
    tq = Q_BLOCK
    rows_per_step = WIN_TILES_PER_STEP * tq
    assert t // tq >= 3 and tq == WINDOW and t % rows_per_step == 0
    tile_blk = pl.BlockSpec((None, rows_per_step, D_MODEL), lambda b, i: (b, i, 0))
    full_blk = lambda rows: pl.BlockSpec((None, rows, D_MODEL), lambda b, i: (b, 0, 0))
    bias_blk = pl.BlockSpec((3, B_GROUP * tq, BAND), lambda b, i: (0, 0, 0))
    const_blk = lambda r, n: pl.BlockSpec((r, n), lambda b, i: (0, 0))
    return pl.pallas_call(
        functools.partial(_win_attn_kernel, final=final),
        grid=(bsz, t // rows_per_step),
        in_specs=[pl.BlockSpec(memory_space=pltpu.SMEM), tile_blk, full_blk(n_ctx),
                  full_blk(n_ctx), full_blk(t), full_blk(t), tile_blk, bias_blk,
                  const_blk(D_MODEL, D_MODEL), tile_blk, _mod_spec(mod),
                  const_blk(1, D_MODEL)],
        out_specs=tile_blk,
        out_shape=jax.ShapeDtypeStruct((bsz, t, D_MODEL), F32),
        scratch_shapes=[pltpu.VMEM((rows_per_step, D_MODEL), BF16)],
        compiler_params=_params(2),
        name="win_attn",
    )(sink, q, kc, vc, kx, vx, gate, _band_bias(tq), w_o, x, mod,
      final_g.reshape(1, D_MODEL))


def _rope_tables(seq):
    pos = jnp.arange(seq, dtype=jnp.int32)
    row = (pos // GRID_W).astype(F32)
    col = (pos % GRID_W).astype(F32)
    axis_dim = HEAD_DIM // 2
    inv_freq = ROPE_THETA ** (-jnp.arange(0, axis_dim, 2, dtype=F32) / axis_dim)
    ang_r = row[:, None] * inv_freq
    ang_c = col[:, None] * inv_freq
    zeros = jnp.zeros_like(ang_r)
    reps = LANES // HEAD_DIM
    cos = jnp.tile(jnp.concatenate(
        [jnp.cos(ang_r), jnp.cos(ang_r), jnp.cos(ang_c), jnp.cos(ang_c)], axis=-1), (1, reps))
    sin_a = jnp.tile(jnp.concatenate(
        [-jnp.sin(ang_r), zeros, -jnp.sin(ang_c), zeros], axis=-1), (1, reps))
    sin_b = jnp.tile(jnp.concatenate(
        [zeros, jnp.sin(ang_r), zeros, jnp.sin(ang_c)], axis=-1), (1, reps))
    return cos, sin_a, sin_b


def _kv_replication_matrix():
    col = jnp.arange(D_MODEL, dtype=jnp.int32)
    src = (col // B_SLAB) * HEAD_DIM + col % HEAD_DIM
    row = jnp.arange(B_KV_HEADS * HEAD_DIM, dtype=jnp.int32)
    return (row[:, None] == src[None, :]).astype(BF16)


def kernel(x, c, ctx, c_ctx, w_mod, b_mod, norm_g, w_o, a_w_in, a_lambda_q1, a_lambda_k1,
           a_lambda_q2, a_lambda_k2, a_subln_g, b_w_in, b_sink, final_g):
    bsz, seq, _ = x.shape
    n_ctx = ctx.shape[1]
    depth = w_mod.shape[0]
    tables = _rope_tables(seq)
    ctx_tables = tuple(tb[:n_ctx] for tb in tables)

    act = jnp.concatenate(
        [c, c_ctx[None, :], jnp.zeros((MOD_ROWS - bsz - 1, D_MODEL), F32)], axis=0)
    mods = _mod_call(act, w_mod, b_mod)

    rep = _kv_replication_matrix()
    kv_w = B_KV_HEADS * HEAD_DIM
    for i in range(depth):
        last = i == depth - 1
        j = i // 2
        mod_x = mods[i, :bsz].reshape(bsz, 1, 3 * D_MODEL)
        mod_c = mods[i, bsz:bsz + 1].reshape(1, 1, 3 * D_MODEL)
        w_out = w_o[i].astype(BF16)
        if i % 2 == 0:
            w = a_w_in[j].astype(BF16)
            groups = tuple((kind, n * D_MODEL) for n, kind in enumerate("qkvg"))
            qx, kx, vx, gx = _proj_call(x, mod_x, norm_g[i], w, rep, tables, groups, True,
                                        512, "proj_a_x")
            qc, kc, vc, gc = _proj_call(ctx, mod_c, norm_g[i], w, rep, ctx_tables, groups,
                                        False, n_ctx, "proj_a_ctx")
            lam_init = 0.8 - 0.6 * math.exp(-0.3 * i)
            lams = (a_lambda_q1[j], a_lambda_k1[j], a_lambda_q2[j], a_lambda_k2[j])
            x = _diff_attn_call(qx, kc, vc, kx, vx, gx, lams, a_subln_g[j], lam_init, w_out,
                                x, mod_x, final_g, last, 256, "diff_attn_x")
            if not last:
                ctx = _diff_attn_call(qc, kc, vc, None, None, gc, lams, a_subln_g[j],
                                      lam_init, w_out, ctx, mod_c, final_g, False, n_ctx,
                                      "diff_attn_ctx")
        else:
            w = b_w_in[j].astype(BF16)
            groups = (("q", 0), ("kv", D_MODEL), ("g", D_MODEL + 2 * kv_w))
            qx, kx, vx, gx = _proj_call(x, mod_x, norm_g[i], w, rep, tables, groups, True,
                                        512, "proj_b_x")
            if last:
                kc, vc = _proj_call(ctx, mod_c, norm_g[i], w, rep, ctx_tables,
                                    (("kv", D_MODEL),), False, n_ctx, "proj_b_ctx")
                x = _win_attn_call(b_sink[j], qx, kc, vc, kx, vx, gx, w_out, x, mod_x,
                                   final_g, True)
            else:
                raise NotImplementedError("context output of a windowed layer")
    return x
```

```python
import functools
import math

import jax
import jax.numpy as jnp
from jax import lax
from jax.experimental import pallas as pl
from jax.experimental.pallas import tpu as pltpu

D_MODEL = 1024
GRID_W = 64
HEAD_DIM = 64
LANES = 128
ONES_ROWS = 16
A_HEADS = D_MODEL // (2 * HEAD_DIM)
B_HEADS = D_MODEL // HEAD_DIM
B_KV_HEADS = 4
B_GROUP = B_HEADS // B_KV_HEADS
B_SLAB = B_GROUP * HEAD_DIM
WINDOW = 128
Q_BLOCK = 128
BAND = Q_BLOCK + 2 * WINDOW
WIN_TILES_PER_STEP = 4
ROPE_THETA = 10000.0
NORM_EPS = 1e-6
SUBLN_EPS = 1e-5
NEG_INF = -1e30
ATTN_SCALE = HEAD_DIM ** -0.5
LOG2E = math.log2(math.e)
MOD_ROWS = 40
VMEM_LIMIT = 56 * 1024 * 1024

F32 = jnp.float32
BF16 = jnp.bfloat16
NT_DIMS = (((1,), (1,)), ((), ()))


def _params(n_grid, flags=None):
    return pltpu.CompilerParams(
        dimension_semantics=("arbitrary",) * n_grid,
        vmem_limit_bytes=VMEM_LIMIT, flags=flags)


def _silu(x):
    return x * (1.0 / (1.0 + jnp.exp(-x)))


def _mod_kernel(act_ref, w_ref, b_ref, out_ref):
    a = _silu(act_ref[...])
    out_ref[...] = jnp.dot(a, w_ref[...], preferred_element_type=F32,
                           precision=lax.Precision.HIGHEST) + b_ref[...]


def _mod_call(act, w_mod, b_mod):
    depth = w_mod.shape[0]
    n_blk = w_mod.shape[2] // D_MODEL
    return pl.pallas_call(
        _mod_kernel,
        grid=(depth, n_blk),
        in_specs=[
            pl.BlockSpec((MOD_ROWS, D_MODEL), lambda l, n: (0, 0)),
            pl.BlockSpec((None, D_MODEL, D_MODEL), lambda l, n: (l, 0, n)),
            pl.BlockSpec((None, 1, D_MODEL), lambda l, n: (l, 0, n)),
        ],
        out_specs=pl.BlockSpec((None, MOD_ROWS, D_MODEL), lambda l, n: (l, 0, n)),
        out_shape=jax.ShapeDtypeStruct((depth, MOD_ROWS, 3 * D_MODEL), F32),
        compiler_params=_params(2),
        name="mod_vectors",
    )(act, w_mod, b_mod.reshape(depth, 1, 3 * D_MODEL))


def _rope_block(r, cos, sin_a, sin_b):
    return (r * cos + pltpu.roll(r, LANES - 16, 1) * sin_a
            + pltpu.roll(r, 16, 1) * sin_b)


def _proj_kernel(x_ref, mod_ref, g_ref, w_ref, aux_ref, cos_ref, sa_ref, sb_ref, *out_refs,
                 groups, rope):
    x = x_ref[...]
    ms = jnp.mean(x * x, axis=-1, keepdims=True)
    y = x * lax.rsqrt(ms + NORM_EPS) * g_ref[...]
    shift = mod_ref[:, 0:D_MODEL]
    scale = mod_ref[:, D_MODEL:2 * D_MODEL]
    h = (y * (1.0 + scale) + shift).astype(BF16)

    def rope_cols(r):
        if not rope:
            return r
        cos, sin_a, sin_b = cos_ref[...], sa_ref[...], sb_ref[...]
        return jnp.concatenate(
            [_rope_block(r[:, hb * LANES:(hb + 1) * LANES], cos, sin_a, sin_b)
             for hb in range(r.shape[1] // LANES)], axis=1)

    outs = list(out_refs)
    for kind, c0 in groups:
        if kind == "kv":
            kv_w = aux_ref.shape[0]
            r = jnp.dot(h, w_ref[:, c0:c0 + 2 * kv_w], preferred_element_type=F32)
            for part in (rope_cols(r[:, :kv_w]), r[:, kv_w:]):
                outs.pop(0)[...] = jnp.dot(part.astype(BF16), aux_ref[...],
                                           preferred_element_type=F32).astype(BF16)
            continue
        if kind == "vt":
            outs.pop(0)[...] = lax.dot_general(
                aux_ref[...], h, NT_DIMS, preferred_element_type=F32).astype(BF16)
            continue
        r = jnp.dot(h, w_ref[:, c0:c0 + D_MODEL], preferred_element_type=F32)
        if kind == "q":
            r = r * (ATTN_SCALE * LOG2E)
        if kind in ("q", "k"):
            r = rope_cols(r)
        outs.pop(0)[...] = r.astype(BF16)


def _proj_call(x, mod, norm_g, w, aux, tables, groups, rope, tm, name):
    bsz, t, _ = x.shape
    per_batch_mod = mod.shape[0] != 1
    mod_map = (lambda b, i: (b, 0, 0)) if per_batch_mod else (lambda b, i: (0, 0, 0))
    row_major = (pl.BlockSpec((None, tm, D_MODEL), lambda b, i: (b, i, 0)),
                 jax.ShapeDtypeStruct((bsz, t, D_MODEL), BF16))
    transposed = (pl.BlockSpec((None, D_MODEL, tm), lambda b, i: (b, 0, i)),
                  jax.ShapeDtypeStruct((bsz, D_MODEL, t), BF16))
    outs = []
    for kind, _ in groups:
        outs += [transposed] if kind == "vt" else [row_major] * (2 if kind == "kv" else 1)
    return pl.pallas_call(
        functools.partial(_proj_kernel, groups=groups, rope=rope),
        grid=(bsz, t // tm),
        in_specs=[
            pl.BlockSpec((None, tm, D_MODEL), lambda b, i: (b, i, 0)),
            pl.BlockSpec((None, 1, 3 * D_MODEL), mod_map),
            pl.BlockSpec((1, D_MODEL), lambda b, i: (0, 0)),
            pl.BlockSpec(w.shape, lambda b, i: (0, 0)),
            pl.BlockSpec(aux.shape, lambda b, i: (0, 0)),
            pl.BlockSpec((tm, LANES), lambda b, i: (i, 0)),
            pl.BlockSpec((tm, LANES), lambda b, i: (i, 0)),
            pl.BlockSpec((tm, LANES), lambda b, i: (i, 0)),
        ],
        out_specs=[spec for spec, _ in outs],
        out_shape=[shape for _, shape in outs],
        compiler_params=_params(2),
        name=name,
    )(x, mod, norm_g.reshape(1, D_MODEL), w, aux, *tables)


def _diff_scores(q_ref, kc_ref, kx_ref, h):
    sl = slice(h * LANES, (h + 1) * LANES)
    q = q_ref[:, sl]
    lane = lax.broadcasted_iota(jnp.int32, q.shape, 1)
    zero = jnp.zeros_like(q)
    qs = jnp.concatenate([jnp.where(lane < HEAD_DIM, q, zero),
                          jnp.where(lane >= HEAD_DIM, q, zero)], axis=0)
    s_c = lax.dot_general(kc_ref[:, sl], qs, NT_DIMS, preferred_element_type=F32)
    s_x = None
    if kx_ref is not None:
        s_x = lax.dot_general(kx_ref[:, sl], qs, NT_DIMS, preferred_element_type=F32)
    return s_c, s_x


def _out_proj_residual(o_scr, w_ref, x_ref, mod_ref, fg_ref, out_ref, final):
    upd = jnp.dot(o_scr[...], w_ref[...], preferred_element_type=F32)
    x = x_ref[...] + mod_ref[:, 2 * D_MODEL:3 * D_MODEL] * upd
    if final:
        ms = jnp.mean(x * x, axis=-1, keepdims=True)
        x = x * lax.rsqrt(ms + NORM_EPS) * fg_ref[...]
    out_ref[...] = x


def _diff_attn_kernel(*refs, with_latent, lam_init, final):
    if with_latent:
        (q_ref, kc_ref, vc_ref, kx_ref, vx_ref, gate_ref, lq1_ref, lk1_ref, lq2_ref,
         lk2_ref, sg_ref, w_ref, x_ref, mod_ref, fg_ref, out_ref, o_scr) = refs
    else:
        (q_ref, kc_ref, vc_ref, gate_ref, lq1_ref, lk1_ref, lq2_ref, lk2_ref, sg_ref,
         w_ref, x_ref, mod_ref, fg_ref, out_ref, o_scr) = refs
        kx_ref = vx_ref = None
    tq = q_ref.shape[0]
    lam = (jnp.exp(jnp.sum(lq1_ref[...] * lk1_ref[...], axis=-1, keepdims=True))
           - jnp.exp(jnp.sum(lq2_ref[...] * lk2_ref[...], axis=-1, keepdims=True))
           + lam_init)
    scores = _diff_scores(q_ref, kc_ref, kx_ref, 0)
    for h in range(A_HEADS):
        sl = slice(h * LANES, (h + 1) * LANES)
        s_c, s_x = scores
        if h + 1 < A_HEADS:
            scores = _diff_scores(q_ref, kc_ref, kx_ref, h + 1)
        m = jnp.max(s_c, axis=0, keepdims=True)
        if with_latent:
            m = jnp.maximum(m, jnp.max(s_x, axis=0, keepdims=True))
        ones = jnp.ones((ONES_ROWS, vc_ref.shape[1]), BF16)
        p_c = jnp.exp2(s_c - m).astype(BF16)
        oa = jnp.dot(jnp.concatenate([vc_ref[sl, :], ones], axis=0), p_c,
                     preferred_element_type=F32)
        if with_latent:
            ones = jnp.ones((ONES_ROWS, vx_ref.shape[1]), BF16)
            p_x = jnp.exp2(s_x - m).astype(BF16)
            oa = oa + jnp.dot(jnp.concatenate([vx_ref[sl, :], ones], axis=0), p_x,
                              preferred_element_type=F32)
        on = oa[:LANES] * (1.0 / oa[LANES:LANES + 1])
        o = on[:, :tq] - lam * on[:, tq:]
        ms = jnp.mean(o * o, axis=0, keepdims=True)
        o = (o * lax.rsqrt(ms + SUBLN_EPS)).T * sg_ref[...] * (1.0 - lam_init)
        o_scr[:, sl] = (o * _silu(gate_ref[:, sl].astype(F32))).astype(BF16)
    _out_proj_residual(o_scr, w_ref, x_ref, mod_ref, fg_ref, out_ref, final)


def _mod_spec(mod):
    if mod.shape[0] != 1:
        return pl.BlockSpec((None, 1, 3 * D_MODEL), lambda b, i: (b, 0, 0))
    return pl.BlockSpec((None, 1, 3 * D_MODEL), lambda b, i: (0, 0, 0))


def _diff_attn_call(q, kc, vc, kx, vx, gate, lams, subln_g, lam_init, w_o, x, mod, final_g,
                    final, tq, name):
    bsz, t, _ = q.shape
    n_ctx = kc.shape[1]
    with_latent = kx is not None
    full_blk = lambda rows: pl.BlockSpec((None, rows, D_MODEL), lambda b, i: (b, 0, 0))
    full_t_blk = lambda cols: pl.BlockSpec((None, D_MODEL, cols), lambda b, i: (b, 0, 0))
    tile_blk = pl.BlockSpec((None, tq, D_MODEL), lambda b, i: (b, i, 0))
    const_blk = lambda r, n: pl.BlockSpec((r, n), lambda b, i: (0, 0))
    in_specs = [tile_blk, full_blk(n_ctx), full_t_blk(n_ctx)]
    args = [q, kc, vc]
    if with_latent:
        in_specs += [full_blk(kx.shape[1]), full_t_blk(kx.shape[1])]
        args += [kx, vx]
    in_specs += ([tile_blk] + [const_blk(1, HEAD_DIM)] * 4 + [const_blk(1, 2 * HEAD_DIM)]
                 + [const_blk(D_MODEL, D_MODEL), tile_blk, _mod_spec(mod),
                    const_blk(1, D_MODEL)])
    args += ([gate] + [v.reshape(1, HEAD_DIM) for v in lams] + [subln_g.reshape(1, -1)]
             + [w_o, x, mod, final_g.reshape(1, D_MODEL)])
    return pl.pallas_call(
        functools.partial(_diff_attn_kernel, with_latent=with_latent, lam_init=lam_init,
                          final=final),
        grid=(bsz, t // tq),
        in_specs=in_specs,
        out_specs=tile_blk,
        out_shape=jax.ShapeDtypeStruct((bsz, t, D_MODEL), F32),
        scratch_shapes=[pltpu.VMEM((tq, D_MODEL), BF16)],
        compiler_params=_params(2),
        name=name,
    )(*args)


def _win_attn_kernel(sink_ref, zero_ref, q_ref, kc_ref, vc_ref, kx_ref, vx_ref, gate_ref,
                     bias_ref, w_ref, x_ref, mod_ref, fg_ref, out_ref, o_scr, s_a, s_b, *,
                     final):
    tq = Q_BLOCK
    tiles = q_ref.shape[0] // tq
    seq = kx_ref.shape[0]
    n_tiles = seq // tq
    n_ctx = kc_ref.shape[0]
    rows = B_GROUP * tq
    row1 = lax.broadcasted_iota(jnp.int32, (rows, 1), 0)
    lane_grp = lax.broadcasted_iota(jnp.int32, (tq, B_SLAB), 1) // HEAD_DIM
    units = [(t, j) for t in range(tiles) for j in range(B_KV_HEADS)]
    z = zero_ref[0]
    slots = (s_a, s_b)

    def window_start(t):
        ti = pl.program_id(1) * tiles + t
        kstart = pl.multiple_of(jnp.clip(ti * tq - WINDOW, 0, seq - BAND), WINDOW)
        return ti, kstart

    def scores(u):
        t, j = units[u]
        ti, kstart = window_start(t)
        sl = slice(j * B_SLAB, (j + 1) * B_SLAB)
        qj = q_ref[t * tq:(t + 1) * tq, sl]
        zero = jnp.zeros_like(qj)
        qs = jnp.concatenate(
            [jnp.where(lane_grp == g, qj, zero) for g in range(B_GROUP)], axis=0)
        slots[u % 2][z, :, :n_ctx] = lax.dot_general(
            qs, kc_ref[:, sl], NT_DIMS, preferred_element_type=F32)
        bias = bias_ref[jnp.where(ti == 0, 0, jnp.where(ti == n_tiles - 1, 2, 1))]
        bias = jnp.concatenate([bias] * B_GROUP, axis=0)
        slots[u % 2][z, :, n_ctx:] = lax.dot_general(
            qs, kx_ref[pl.ds(kstart, BAND), sl], NT_DIMS, preferred_element_type=F32) + bias

    scores(0)
    for u, (t, j) in enumerate(units):
        if u + 1 < len(units):
            scores(u + 1)
        _, kstart = window_start(t)
        rs = slice(t * tq, (t + 1) * tq)
        sl = slice(j * B_SLAB, (j + 1) * B_SLAB)
        sink = jnp.full((rows, 1), sink_ref[j * B_GROUP + B_GROUP - 1], F32)
        for g in range(B_GROUP - 2, -1, -1):
            sink = jnp.where(row1 < (g + 1) * tq, sink_ref[j * B_GROUP + g], sink)
        sink = sink * LOG2E
        m = jnp.maximum(jnp.max(slots[u % 2][z], axis=-1, keepdims=True), sink)
        e = jnp.exp2(slots[u % 2][z] - m)
        l = jnp.sum(e, axis=-1, keepdims=True) + jnp.exp2(sink - m)
        e = e.astype(BF16)
        o = (jnp.dot(e[:, :n_ctx], vc_ref[:, sl], preferred_element_type=F32)
             + jnp.dot(e[:, n_ctx:], vx_ref[pl.ds(kstart, BAND), sl],
                       preferred_element_type=F32))
        o = o * (1.0 / l)
        o_j = o[(B_GROUP - 1) * tq:]
        for g in range(B_GROUP - 2, -1, -1):
            o_j = jnp.where(lane_grp == g, o[g * tq:(g + 1) * tq], o_j)
        o_scr[rs, sl] = (o_j * _silu(gate_ref[rs, sl].astype(F32))).astype(BF16)
    _out_proj_residual(o_scr, w_ref, x_ref, mod_ref, fg_ref, out_ref, final)


def _band_bias(tq):
    r = jnp.arange(tq, dtype=jnp.int32)[None, :, None]
    c = jnp.arange(BAND, dtype=jnp.int32)[None, None, :]
    off = (jnp.arange(3, dtype=jnp.int32) * WINDOW)[:, None, None]
    return jnp.where(jnp.abs(c - off - r) <= WINDOW, 0.0, NEG_INF).astype(F32)


def _win_attn_call(sink, q, kc, vc, kx, vx, gate, w_o, x, mod, final_g, final):
    bsz, t, _ = q.shape
    n_ctx = kc.shape[1]
    tq = Q_BLOCK
    rows_per_step = WIN_TILES_PER_STEP * tq
    assert t // tq >= 3 and tq == WINDOW and t % rows_per_step == 0
    tile_blk = pl.BlockSpec((None, rows_per_step, D_MODEL), lambda b, i: (b, i, 0))
    full_blk = lambda rows: pl.BlockSpec((None, rows, D_MODEL), lambda b, i: (b, 0, 0))
    bias_blk = pl.BlockSpec((3, tq, BAND), lambda b, i: (0, 0, 0))
    const_blk = lambda r, n: pl.BlockSpec((r, n), lambda b, i: (0, 0))
    return pl.pallas_call(
        functools.partial(_win_attn_kernel, final=final),
        grid=(bsz, t // rows_per_step),
        in_specs=[pl.BlockSpec(memory_space=pltpu.SMEM),
                  pl.BlockSpec(memory_space=pltpu.SMEM), tile_blk, full_blk(n_ctx),
                  full_blk(n_ctx), full_blk(t), full_blk(t), tile_blk, bias_blk,
                  const_blk(D_MODEL, D_MODEL), tile_blk, _mod_spec(mod),
                  const_blk(1, D_MODEL)],
        out_specs=tile_blk,
        out_shape=jax.ShapeDtypeStruct((bsz, t, D_MODEL), F32),
        scratch_shapes=[pltpu.VMEM((rows_per_step, D_MODEL), BF16)]
        + [pltpu.VMEM((1, B_GROUP * tq, n_ctx + BAND), F32)] * 2,
        compiler_params=_params(2),
        name="win_attn",
    )(sink, jnp.zeros((1,), jnp.int32), q, kc, vc, kx, vx, gate, _band_bias(tq), w_o, x,
      mod, final_g.reshape(1, D_MODEL))


def _rope_tables(seq):
    pos = jnp.arange(seq, dtype=jnp.int32)
    row = (pos // GRID_W).astype(F32)
    col = (pos % GRID_W).astype(F32)
    axis_dim = HEAD_DIM // 2
    inv_freq = ROPE_THETA ** (-jnp.arange(0, axis_dim, 2, dtype=F32) / axis_dim)
    ang_r = row[:, None] * inv_freq
    ang_c = col[:, None] * inv_freq
    zeros = jnp.zeros_like(ang_r)
    reps = LANES // HEAD_DIM
    cos = jnp.tile(jnp.concatenate(
        [jnp.cos(ang_r), jnp.cos(ang_r), jnp.cos(ang_c), jnp.cos(ang_c)], axis=-1), (1, reps))
    sin_a = jnp.tile(jnp.concatenate(
        [-jnp.sin(ang_r), zeros, -jnp.sin(ang_c), zeros], axis=-1), (1, reps))
    sin_b = jnp.tile(jnp.concatenate(
        [zeros, jnp.sin(ang_r), zeros, jnp.sin(ang_c)], axis=-1), (1, reps))
    return cos, sin_a, sin_b


def _kv_replication_matrix():
    col = jnp.arange(D_MODEL, dtype=jnp.int32)
    src = (col // B_SLAB) * HEAD_DIM + col % HEAD_DIM
    row = jnp.arange(B_KV_HEADS * HEAD_DIM, dtype=jnp.int32)
    return (row[:, None] == src[None, :]).astype(BF16)


def kernel(x, c, ctx, c_ctx, w_mod, b_mod, norm_g, w_o, a_w_in, a_lambda_q1, a_lambda_k1,
           a_lambda_q2, a_lambda_k2, a_subln_g, b_w_in, b_sink, final_g):
    bsz, seq, _ = x.shape
    n_ctx = ctx.shape[1]
    depth = w_mod.shape[0]
    tables = _rope_tables(seq)
    ctx_tables = tuple(tb[:n_ctx] for tb in tables)

    act = jnp.concatenate(
        [c, c_ctx[None, :], jnp.zeros((MOD_ROWS - bsz - 1, D_MODEL), F32)], axis=0)
    mods = _mod_call(act, w_mod, b_mod)

    rep = _kv_replication_matrix()
    kv_w = B_KV_HEADS * HEAD_DIM
    for i in range(depth):
        last = i == depth - 1
        j = i // 2
        mod_x = mods[i, :bsz].reshape(bsz, 1, 3 * D_MODEL)
        mod_c = mods[i, bsz:bsz + 1].reshape(1, 1, 3 * D_MODEL)
        w_out = w_o[i].astype(BF16)
        if i % 2 == 0:
            w = a_w_in[j].astype(BF16)
            w_vt = a_w_in[j][:, 2 * D_MODEL:3 * D_MODEL].T.astype(BF16)
            groups = (("q", 0), ("k", D_MODEL), ("vt", 2 * D_MODEL), ("g", 3 * D_MODEL))
            qx, kx, vx, gx = _proj_call(x, mod_x, norm_g[i], w, w_vt, tables, groups, True,
                                        512, "proj_a_x")
            qc, kc, vc, gc = _proj_call(ctx, mod_c, norm_g[i], w, w_vt, ctx_tables, groups,
                                        False, n_ctx, "proj_a_ctx")
            lam_init = 0.8 - 0.6 * math.exp(-0.3 * i)
            lams = (a_lambda_q1[j], a_lambda_k1[j], a_lambda_q2[j], a_lambda_k2[j])
            x = _diff_attn_call(qx, kc, vc, kx, vx, gx, lams, a_subln_g[j], lam_init, w_out,
                                x, mod_x, final_g, last, 256, "diff_attn_x")
            if not last:
                ctx = _diff_attn_call(qc, kc, vc, None, None, gc, lams, a_subln_g[j],
                                      lam_init, w_out, ctx, mod_c, final_g, False, n_ctx,
                                      "diff_attn_ctx")
        else:
            w = b_w_in[j].astype(BF16)
            groups = (("q", 0), ("kv", D_MODEL), ("g", D_MODEL + 2 * kv_w))
            qx, kx, vx, gx = _proj_call(x, mod_x, norm_g[i], w, rep, tables, groups, True,
                                        512, "proj_b_x")
            if last:
                kc, vc = _proj_call(ctx, mod_c, norm_g[i], w, rep, ctx_tables,
                                    (("kv", D_MODEL),), False, n_ctx, "proj_b_ctx")
                x = _win_attn_call(b_sink[j], qx, kc, vc, kx, vx, gx, w_out, x, mod_x,
                                   final_g, True)
            else:
                raise NotImplementedError("context output of a windowed layer")
    return x
```

```python
import functools
import math

import jax
import jax.numpy as jnp
from jax import lax
from jax.experimental import pallas as pl
from jax.experimental.pallas import tpu as pltpu

D_MODEL = 1024
GRID_W = 64
HEAD_DIM = 64
LANES = 128
A_HEADS = D_MODEL // (2 * HEAD_DIM)
B_HEADS = D_MODEL // HEAD_DIM
B_KV_HEADS = 4
B_GROUP = B_HEADS // B_KV_HEADS
B_SLAB = B_GROUP * HEAD_DIM
WINDOW = 128
Q_BLOCK = 128
BAND = Q_BLOCK + 2 * WINDOW
WIN_TILES_PER_STEP = 4
ROPE_THETA = 10000.0
NORM_EPS = 1e-6
SUBLN_EPS = 1e-5
NEG_INF = -1e30
ATTN_SCALE = HEAD_DIM ** -0.5
LOG2E = math.log2(math.e)
MOD_ROWS = 40
VMEM_LIMIT = 56 * 1024 * 1024

F32 = jnp.float32
BF16 = jnp.bfloat16
NT_DIMS = (((1,), (1,)), ((), ()))


def _params(n_grid, flags=None):
    return pltpu.CompilerParams(
        dimension_semantics=("arbitrary",) * n_grid,
        vmem_limit_bytes=VMEM_LIMIT, flags=flags)


def _silu(x):
    return x * (1.0 / (1.0 + jnp.exp(-x)))


def _mod_kernel(act_ref, w_ref, b_ref, out_ref):
    a = _silu(act_ref[...])
    out_ref[...] = jnp.dot(a, w_ref[...], preferred_element_type=F32,
                           precision=lax.Precision.HIGHEST) + b_ref[...]


def _mod_call(act, w_mod, b_mod):
    depth = w_mod.shape[0]
    n_blk = w_mod.shape[2] // D_MODEL
    return pl.pallas_call(
        _mod_kernel,
        grid=(depth, n_blk),
        in_specs=[
            pl.BlockSpec((MOD_ROWS, D_MODEL), lambda l, n: (0, 0)),
            pl.BlockSpec((None, D_MODEL, D_MODEL), lambda l, n: (l, 0, n)),
            pl.BlockSpec((None, 1, D_MODEL), lambda l, n: (l, 0, n)),
        ],
        out_specs=pl.BlockSpec((None, MOD_ROWS, D_MODEL), lambda l, n: (l, 0, n)),
        out_shape=jax.ShapeDtypeStruct((depth, MOD_ROWS, 3 * D_MODEL), F32),
        compiler_params=_params(2),
        name="mod_vectors",
    )(act, w_mod, b_mod.reshape(depth, 1, 3 * D_MODEL))


def _rope_block(r, cos, sin_a, sin_b):
    return (r * cos + pltpu.roll(r, LANES - 16, 1) * sin_a
            + pltpu.roll(r, 16, 1) * sin_b)


def _proj_kernel(x_ref, mod_ref, g_ref, w_ref, aux_ref, cos_ref, sa_ref, sb_ref, *out_refs,
                 groups, rope):
    x = x_ref[...]
    ms = jnp.mean(x * x, axis=-1, keepdims=True)
    y = x * lax.rsqrt(ms + NORM_EPS) * g_ref[...]
    shift = mod_ref[:, 0:D_MODEL]
    scale = mod_ref[:, D_MODEL:2 * D_MODEL]
    h = (y * (1.0 + scale) + shift).astype(BF16)

    def rope_cols(r):
        if not rope:
            return r
        cos, sin_a, sin_b = cos_ref[...], sa_ref[...], sb_ref[...]
        return jnp.concatenate(
            [_rope_block(r[:, hb * LANES:(hb + 1) * LANES], cos, sin_a, sin_b)
             for hb in range(r.shape[1] // LANES)], axis=1)

    outs = list(out_refs)
    for kind, c0 in groups:
        if kind == "kv":
            kv_w = aux_ref.shape[0]
            r = jnp.dot(h, w_ref[:, c0:c0 + 2 * kv_w], preferred_element_type=F32)
            for part in (rope_cols(r[:, :kv_w]), r[:, kv_w:]):
                outs.pop(0)[...] = jnp.dot(part.astype(BF16), aux_ref[...],
                                           preferred_element_type=F32).astype(BF16)
            continue
        r = jnp.dot(h, w_ref[:, c0:c0 + D_MODEL], preferred_element_type=F32)
        if kind == "q":
            r = r * (ATTN_SCALE * LOG2E)
        if kind in ("q", "k"):
            r = rope_cols(r)
        outs.pop(0)[...] = r.astype(BF16)


def _proj_call(x, mod, norm_g, w, aux, tables, groups, rope, tm, name):
    bsz, t, _ = x.shape
    per_batch_mod = mod.shape[0] != 1
    mod_map = (lambda b, i: (b, 0, 0)) if per_batch_mod else (lambda b, i: (0, 0, 0))
    n_out = sum(2 if kind == "kv" else 1 for kind, _ in groups)
    return pl.pallas_call(
        functools.partial(_proj_kernel, groups=groups, rope=rope),
        grid=(bsz, t // tm),
        in_specs=[
            pl.BlockSpec((None, tm, D_MODEL), lambda b, i: (b, i, 0)),
            pl.BlockSpec((None, 1, 3 * D_MODEL), mod_map),
            pl.BlockSpec((1, D_MODEL), lambda b, i: (0, 0)),
            pl.BlockSpec(w.shape, lambda b, i: (0, 0)),
            pl.BlockSpec(aux.shape, lambda b, i: (0, 0)),
            pl.BlockSpec((tm, LANES), lambda b, i: (i, 0)),
            pl.BlockSpec((tm, LANES), lambda b, i: (i, 0)),
            pl.BlockSpec((tm, LANES), lambda b, i: (i, 0)),
        ],
        out_specs=[pl.BlockSpec((None, tm, D_MODEL), lambda b, i: (b, i, 0))] * n_out,
        out_shape=[jax.ShapeDtypeStruct((bsz, t, D_MODEL), BF16)] * n_out,
        compiler_params=_params(2),
        name=name,
    )(x, mod, norm_g.reshape(1, D_MODEL), w, aux, *tables)


def _diff_scores(q_ref, kc_ref, kx_ref, h, slot_x):
    sl = slice(h * LANES, (h + 1) * LANES)
    q = q_ref[:, sl]
    lane = lax.broadcasted_iota(jnp.int32, q.shape, 1)
    zero = jnp.zeros_like(q)
    qs = jnp.concatenate([jnp.where(lane < HEAD_DIM, q, zero),
                          jnp.where(lane >= HEAD_DIM, q, zero)], axis=0)
    if kx_ref is not None:
        slot_x[...] = lax.dot_general(qs, kx_ref[:, sl], NT_DIMS, preferred_element_type=F32)
    return lax.dot_general(qs, kc_ref[:, sl], NT_DIMS, preferred_element_type=F32)


def _out_proj_residual(o_scr, w_ref, x_ref, mod_ref, fg_ref, out_ref, final):
    upd = jnp.dot(o_scr[...], w_ref[...], preferred_element_type=F32)
    x = x_ref[...] + mod_ref[:, 2 * D_MODEL:3 * D_MODEL] * upd
    if final:
        ms = jnp.mean(x * x, axis=-1, keepdims=True)
        x = x * lax.rsqrt(ms + NORM_EPS) * fg_ref[...]
    out_ref[...] = x


def _diff_attn_kernel(*refs, with_latent, lam_init, final):
    if with_latent:
        (zero_ref, q_ref, kc_ref, vc_ref, kx_ref, vx_ref, gate_ref, lq1_ref, lk1_ref,
         lq2_ref, lk2_ref, sg_ref, w_ref, x_ref, mod_ref, fg_ref, out_ref, o_scr, s_a,
         s_b) = refs
        z = zero_ref[0]
        slots = (s_a.at[z], s_b.at[z])
    else:
        (zero_ref, q_ref, kc_ref, vc_ref, gate_ref, lq1_ref, lk1_ref, lq2_ref, lk2_ref,
         sg_ref, w_ref, x_ref, mod_ref, fg_ref, out_ref, o_scr) = refs
        kx_ref = vx_ref = None
        slots = (None, None)
    tq = q_ref.shape[0]
    lam = (jnp.exp(jnp.sum(lq1_ref[...] * lk1_ref[...], axis=-1, keepdims=True))
           - jnp.exp(jnp.sum(lq2_ref[...] * lk2_ref[...], axis=-1, keepdims=True))
           + lam_init)
    s_next = _diff_scores(q_ref, kc_ref, kx_ref, 0, slots[0])
    for h in range(A_HEADS):
        sl = slice(h * LANES, (h + 1) * LANES)
        s_c, s_x = s_next, slots[h % 2]
        if h + 1 < A_HEADS:
            s_next = _diff_scores(q_ref, kc_ref, kx_ref, h + 1, slots[(h + 1) % 2])
        m = jnp.max(s_c, axis=-1, keepdims=True)
        if with_latent:
            m = jnp.maximum(m, jnp.max(s_x[...], axis=-1, keepdims=True))
        ones = jnp.ones((vc_ref.shape[0], LANES), BF16)
        p_c = jnp.exp2(s_c - m).astype(BF16)
        oa = jnp.dot(p_c, jnp.concatenate([vc_ref[:, sl], ones], axis=1),
                     preferred_element_type=F32)
        if with_latent:
            ones = jnp.ones((vx_ref.shape[0], LANES), BF16)
            p_x = jnp.exp2(s_x[...] - m).astype(BF16)
            oa = oa + jnp.dot(p_x, jnp.concatenate([vx_ref[:, sl], ones], axis=1),
                              preferred_element_type=F32)
        on = oa[:, :LANES] * (1.0 / oa[:, LANES:])
        o = on[:tq] - lam * on[tq:]
        ms = jnp.mean(o * o, axis=-1, keepdims=True)
        o = o * lax.rsqrt(ms + SUBLN_EPS) * sg_ref[...] * (1.0 - lam_init)
        o_scr[:, sl] = (o * _silu(gate_ref[:, sl].astype(F32))).astype(BF16)
    _out_proj_residual(o_scr, w_ref, x_ref, mod_ref, fg_ref, out_ref, final)


def _mod_spec(mod):
    if mod.shape[0] != 1:
        return pl.BlockSpec((None, 1, 3 * D_MODEL), lambda b, i: (b, 0, 0))
    return pl.BlockSpec((None, 1, 3 * D_MODEL), lambda b, i: (0, 0, 0))


def _diff_attn_call(q, kc, vc, kx, vx, gate, lams, subln_g, lam_init, w_o, x, mod, final_g,
                    final, tq, name):
    bsz, t, _ = q.shape
    n_ctx = kc.shape[1]
    with_latent = kx is not None
    full_blk = lambda rows: pl.BlockSpec((None, rows, D_MODEL), lambda b, i: (b, 0, 0))
    tile_blk = pl.BlockSpec((None, tq, D_MODEL), lambda b, i: (b, i, 0))
    const_blk = lambda r, n: pl.BlockSpec((r, n), lambda b, i: (0, 0))
    in_specs = [pl.BlockSpec(memory_space=pltpu.SMEM), tile_blk, full_blk(n_ctx),
                full_blk(n_ctx)]
    args = [jnp.zeros((1,), jnp.int32), q, kc, vc]
    scratch = [pltpu.VMEM((tq, D_MODEL), BF16)]
    if with_latent:
        in_specs += [full_blk(kx.shape[1]), full_blk(kx.shape[1])]
        args += [kx, vx]
        scratch += [pltpu.VMEM((1, 2 * tq, kx.shape[1]), F32)] * 2
    in_specs += ([tile_blk] + [const_blk(1, HEAD_DIM)] * 4 + [const_blk(1, 2 * HEAD_DIM)]
                 + [const_blk(D_MODEL, D_MODEL), tile_blk, _mod_spec(mod),
                    const_blk(1, D_MODEL)])
    args += ([gate] + [v.reshape(1, HEAD_DIM) for v in lams] + [subln_g.reshape(1, -1)]
             + [w_o, x, mod, final_g.reshape(1, D_MODEL)])
    return pl.pallas_call(
        functools.partial(_diff_attn_kernel, with_latent=with_latent, lam_init=lam_init,
                          final=final),
        grid=(bsz, t // tq),
        in_specs=in_specs,
        out_specs=tile_blk,
        out_shape=jax.ShapeDtypeStruct((bsz, t, D_MODEL), F32),
        scratch_shapes=scratch,
        compiler_params=_params(2),
        name=name,
    )(*args)


def _win_attn_kernel(sink_ref, zero_ref, q_ref, kc_ref, vc_ref, kx_ref, vx_ref, gate_ref,
                     bias_ref, w_ref, x_ref, mod_ref, fg_ref, out_ref, o_scr, s_a, s_b, *,
                     final):
    tq = Q_BLOCK
    tiles = q_ref.shape[0] // tq
    seq = kx_ref.shape[0]
    n_tiles = seq // tq
    n_ctx = kc_ref.shape[0]
    rows = B_GROUP * tq
    row1 = lax.broadcasted_iota(jnp.int32, (rows, 1), 0)
    lane_grp = lax.broadcasted_iota(jnp.int32, (tq, B_SLAB), 1) // HEAD_DIM
    units = [(t, j) for t in range(tiles) for j in range(B_KV_HEADS)]
    z = zero_ref[0]
    slots = (s_a, s_b)

    def window_start(t):
        ti = pl.program_id(1) * tiles + t
        kstart = pl.multiple_of(jnp.clip(ti * tq - WINDOW, 0, seq - BAND), WINDOW)
        return ti, kstart

    def scores(u):
        t, j = units[u]
        ti, kstart = window_start(t)
        sl = slice(j * B_SLAB, (j + 1) * B_SLAB)
        qj = q_ref[t * tq:(t + 1) * tq, sl]
        zero = jnp.zeros_like(qj)
        qs = jnp.concatenate(
            [jnp.where(lane_grp == g, qj, zero) for g in range(B_GROUP)], axis=0)
        slots[u % 2][z, :, :n_ctx] = lax.dot_general(
            qs, kc_ref[:, sl], NT_DIMS, preferred_element_type=F32)
        bias = bias_ref[jnp.where(ti == 0, 0, jnp.where(ti == n_tiles - 1, 2, 1))]
        bias = jnp.concatenate([bias] * B_GROUP, axis=0)
        slots[u % 2][z, :, n_ctx:] = lax.dot_general(
            qs, kx_ref[pl.ds(kstart, BAND), sl], NT_DIMS, preferred_element_type=F32) + bias

    scores(0)
    for u, (t, j) in enumerate(units):
        if u + 1 < len(units):
            scores(u + 1)
        _, kstart = window_start(t)
        rs = slice(t * tq, (t + 1) * tq)
        sl = slice(j * B_SLAB, (j + 1) * B_SLAB)
        sink = jnp.full((rows, 1), sink_ref[j * B_GROUP + B_GROUP - 1], F32)
        for g in range(B_GROUP - 2, -1, -1):
            sink = jnp.where(row1 < (g + 1) * tq, sink_ref[j * B_GROUP + g], sink)
        sink = sink * LOG2E
        m = jnp.maximum(jnp.max(slots[u % 2][z], axis=-1, keepdims=True), sink)
        e = jnp.exp2(slots[u % 2][z] - m)
        l = jnp.sum(e, axis=-1, keepdims=True) + jnp.exp2(sink - m)
        e = e.astype(BF16)
        o = (jnp.dot(e[:, :n_ctx], vc_ref[:, sl], preferred_element_type=F32)
             + jnp.dot(e[:, n_ctx:], vx_ref[pl.ds(kstart, BAND), sl],
                       preferred_element_type=F32))
        o = o * (1.0 / l)
        o_j = o[(B_GROUP - 1) * tq:]
        for g in range(B_GROUP - 2, -1, -1):
            o_j = jnp.where(lane_grp == g, o[g * tq:(g + 1) * tq], o_j)
        o_scr[rs, sl] = (o_j * _silu(gate_ref[rs, sl].astype(F32))).astype(BF16)
    _out_proj_residual(o_scr, w_ref, x_ref, mod_ref, fg_ref, out_ref, final)


def _band_bias(tq):
    r = jnp.arange(tq, dtype=jnp.int32)[None, :, None]
    c = jnp.arange(BAND, dtype=jnp.int32)[None, None, :]
    off = (jnp.arange(3, dtype=jnp.int32) * WINDOW)[:, None, None]
    return jnp.where(jnp.abs(c - off - r) <= WINDOW, 0.0, NEG_INF).astype(F32)


def _win_attn_call(sink, q, kc, vc, kx, vx, gate, w_o, x, mod, final_g, final):
    bsz, t, _ = q.shape
    n_ctx = kc.shape[1]
    tq = Q_BLOCK
    rows_per_step = WIN_TILES_PER_STEP * tq
    assert t // tq >= 3 and tq == WINDOW and t % rows_per_step == 0
    tile_blk = pl.BlockSpec((None, rows_per_step, D_MODEL), lambda b, i: (b, i, 0))
    full_blk = lambda rows: pl.BlockSpec((None, rows, D_MODEL), lambda b, i: (b, 0, 0))
    bias_blk = pl.BlockSpec((3, tq, BAND), lambda b, i: (0, 0, 0))
    const_blk = lambda r, n: pl.BlockSpec((r, n), lambda b, i: (0, 0))
    return pl.pallas_call(
        functools.partial(_win_attn_kernel, final=final),
        grid=(bsz, t // rows_per_step),
        in_specs=[pl.BlockSpec(memory_space=pltpu.SMEM),
                  pl.BlockSpec(memory_space=pltpu.SMEM), tile_blk, full_blk(n_ctx),
                  full_blk(n_ctx), full_blk(t), full_blk(t), tile_blk, bias_blk,
                  const_blk(D_MODEL, D_MODEL), tile_blk, _mod_spec(mod),
                  const_blk(1, D_MODEL)],
        out_specs=tile_blk,
        out_shape=jax.ShapeDtypeStruct((bsz, t, D_MODEL), F32),
        scratch_shapes=[pltpu.VMEM((rows_per_step, D_MODEL), BF16)]
        + [pltpu.VMEM((1, B_GROUP * tq, n_ctx + BAND), F32)] * 2,
        compiler_params=_params(2),
        name="win_attn",
    )(sink, jnp.zeros((1,), jnp.int32), q, kc, vc, kx, vx, gate, _band_bias(tq), w_o, x,
      mod, final_g.reshape(1, D_MODEL))


def _rope_tables(seq):
    pos = jnp.arange(seq, dtype=jnp.int32)
    row = (pos // GRID_W).astype(F32)
    col = (pos % GRID_W).astype(F32)
    axis_dim = HEAD_DIM // 2
    inv_freq = ROPE_THETA ** (-jnp.arange(0, axis_dim, 2, dtype=F32) / axis_dim)
    ang_r = row[:, None] * inv_freq
    ang_c = col[:, None] * inv_freq
    zeros = jnp.zeros_like(ang_r)
    reps = LANES // HEAD_DIM
    cos = jnp.tile(jnp.concatenate(
        [jnp.cos(ang_r), jnp.cos(ang_r), jnp.cos(ang_c), jnp.cos(ang_c)], axis=-1), (1, reps))
    sin_a = jnp.tile(jnp.concatenate(
        [-jnp.sin(ang_r), zeros, -jnp.sin(ang_c), zeros], axis=-1), (1, reps))
    sin_b = jnp.tile(jnp.concatenate(
        [zeros, jnp.sin(ang_r), zeros, jnp.sin(ang_c)], axis=-1), (1, reps))
    return cos, sin_a, sin_b


def _kv_replication_matrix():
    col = jnp.arange(D_MODEL, dtype=jnp.int32)
    src = (col // B_SLAB) * HEAD_DIM + col % HEAD_DIM
    row = jnp.arange(B_KV_HEADS * HEAD_DIM, dtype=jnp.int32)
    return (row[:, None] == src[None, :]).astype(BF16)


def kernel(x, c, ctx, c_ctx, w_mod, b_mod, norm_g, w_o, a_w_in, a_lambda_q1, a_lambda_k1,
           a_lambda_q2, a_lambda_k2, a_subln_g, b_w_in, b_sink, final_g):
    bsz, seq, _ = x.shape
    n_ctx = ctx.shape[1]
    depth = w_mod.shape[0]
    tables = _rope_tables(seq)
    ctx_tables = tuple(tb[:n_ctx] for tb in tables)

    act = jnp.concatenate(
        [c, c_ctx[None, :], jnp.zeros((MOD_ROWS - bsz - 1, D_MODEL), F32)], axis=0)
    mods = _mod_call(act, w_mod, b_mod)

    rep = _kv_replication_matrix()
    kv_w = B_KV_HEADS * HEAD_DIM
    for i in range(depth):
        last = i == depth - 1
        j = i // 2
        mod_x = mods[i, :bsz].reshape(bsz, 1, 3 * D_MODEL)
        mod_c = mods[i, bsz:bsz + 1].reshape(1, 1, 3 * D_MODEL)
        w_out = w_o[i].astype(BF16)
        if i % 2 == 0:
            w = a_w_in[j].astype(BF16)
            groups = tuple((kind, n * D_MODEL) for n, kind in enumerate("qkvg"))
            qx, kx, vx, gx = _proj_call(x, mod_x, norm_g[i], w, rep, tables, groups, True,
                                        512, "proj_a_x")
            qc, kc, vc, gc = _proj_call(ctx, mod_c, norm_g[i], w, rep, ctx_tables, groups,
                                        False, n_ctx, "proj_a_ctx")
            lam_init = 0.8 - 0.6 * math.exp(-0.3 * i)
            lams = (a_lambda_q1[j], a_lambda_k1[j], a_lambda_q2[j], a_lambda_k2[j])
            x = _diff_attn_call(qx, kc, vc, kx, vx, gx, lams, a_subln_g[j], lam_init, w_out,
                                x, mod_x, final_g, last, 256, "diff_attn_x")
            if not last:
                ctx = _diff_attn_call(qc, kc, vc, None, None, gc, lams, a_subln_g[j],
                                      lam_init, w_out, ctx, mod_c, final_g, False, n_ctx,
                                      "diff_attn_ctx")
        else:
            w = b_w_in[j].astype(BF16)
            groups = (("q", 0), ("kv", D_MODEL), ("g", D_MODEL + 2 * kv_w))
            qx, kx, vx, gx = _proj_call(x, mod_x, norm_g[i], w, rep, tables, groups, True,
                                        1024, "proj_b_x")
            if last:
                kc, vc = _proj_call(ctx, mod_c, norm_g[i], w, rep, ctx_tables,
                                    (("kv", D_MODEL),), False, n_ctx, "proj_b_ctx")
                x = _win_attn_call(b_sink[j], qx, kc, vc, kx, vx, gx, w_out, x, mod_x,
                                   final_g, True)
            else:
                raise NotImplementedError("context output of a windowed layer")
    return x
```

```python
import functools
import math

import jax
import jax.numpy as jnp
from jax import lax
from jax.experimental import pallas as pl
from jax.experimental.pallas import tpu as pltpu

D_MODEL = 1024
GRID_W = 64
HEAD_DIM = 64
LANES = 128
A_HEADS = D_MODEL // (2 * HEAD_DIM)
B_HEADS = D_MODEL // HEAD_DIM
B_KV_HEADS = 4
B_GROUP = B_HEADS // B_KV_HEADS
B_SLAB = B_GROUP * HEAD_DIM
WINDOW = 128
Q_BLOCK = 128
BAND = Q_BLOCK + 2 * WINDOW
WIN_TILES_PER_STEP = 4
ROPE_THETA = 10000.0
NORM_EPS = 1e-6
SUBLN_EPS = 1e-5
NEG_INF = -1e30
ATTN_SCALE = HEAD_DIM ** -0.5
LOG2E = math.log2(math.e)
MOD_ROWS = 40
VMEM_LIMIT = 56 * 1024 * 1024

F32 = jnp.float32
BF16 = jnp.bfloat16
NT_DIMS = (((1,), (1,)), ((), ()))


def _params(n_grid, flags=None):
    return pltpu.CompilerParams(
        dimension_semantics=("arbitrary",) * n_grid,
        vmem_limit_bytes=VMEM_LIMIT, flags=flags)


def _silu(x):
    return x * (1.0 / (1.0 + jnp.exp(-x)))


def _mod_kernel(act_ref, w_ref, b_ref, out_ref):
    a = _silu(act_ref[...])
    out_ref[...] = jnp.dot(a, w_ref[...], preferred_element_type=F32,
                           precision=lax.Precision.HIGHEST) + b_ref[...]


def _mod_call(act, w_mod, b_mod):
    depth = w_mod.shape[0]
    n_blk = w_mod.shape[2] // D_MODEL
    return pl.pallas_call(
        _mod_kernel,
        grid=(depth, n_blk),
        in_specs=[
            pl.BlockSpec((MOD_ROWS, D_MODEL), lambda l, n: (0, 0)),
            pl.BlockSpec((None, D_MODEL, D_MODEL), lambda l, n: (l, 0, n)),
            pl.BlockSpec((None, 1, D_MODEL), lambda l, n: (l, 0, n)),
        ],
        out_specs=pl.BlockSpec((None, MOD_ROWS, D_MODEL), lambda l, n: (l, 0, n)),
        out_shape=jax.ShapeDtypeStruct((depth, MOD_ROWS, 3 * D_MODEL), F32),
        compiler_params=_params(2),
        name="mod_vectors",
    )(act, w_mod, b_mod.reshape(depth, 1, 3 * D_MODEL))


def _rope_block(r, cos, sin_a, sin_b):
    return (r * cos + pltpu.roll(r, LANES - 16, 1) * sin_a
            + pltpu.roll(r, 16, 1) * sin_b)


def _proj_kernel(x_ref, mod_ref, g_ref, w_ref, aux_ref, cos_ref, sa_ref, sb_ref, *out_refs,
                 groups, rope):
    x = x_ref[...]
    ms = jnp.mean(x * x, axis=-1, keepdims=True)
    y = x * lax.rsqrt(ms + NORM_EPS) * g_ref[...]
    shift = mod_ref[:, 0:D_MODEL]
    scale = mod_ref[:, D_MODEL:2 * D_MODEL]
    h = (y * (1.0 + scale) + shift).astype(BF16)

    def rope_cols(r):
        if not rope:
            return r
        cos, sin_a, sin_b = cos_ref[...], sa_ref[...], sb_ref[...]
        return jnp.concatenate(
            [_rope_block(r[:, hb * LANES:(hb + 1) * LANES], cos, sin_a, sin_b)
             for hb in range(r.shape[1] // LANES)], axis=1)

    outs = list(out_refs)
    for kind, c0 in groups:
        if kind == "kv":
            kv_w = aux_ref.shape[0]
            r = jnp.dot(h, w_ref[:, c0:c0 + 2 * kv_w], preferred_element_type=F32)
            for part in (rope_cols(r[:, :kv_w]), r[:, kv_w:]):
                outs.pop(0)[...] = jnp.dot(part.astype(BF16), aux_ref[...],
                                           preferred_element_type=F32).astype(BF16)
            continue
        r = jnp.dot(h, w_ref[:, c0:c0 + D_MODEL], preferred_element_type=F32)
        if kind == "q":
            r = r * (ATTN_SCALE * LOG2E)
        if kind in ("q", "k"):
            r = rope_cols(r)
        outs.pop(0)[...] = r.astype(BF16)


def _proj_call(x, mod, norm_g, w, aux, tables, groups, rope, tm, name):
    bsz, t, _ = x.shape
    per_batch_mod = mod.shape[0] != 1
    mod_map = (lambda b, i: (b, 0, 0)) if per_batch_mod else (lambda b, i: (0, 0, 0))
    n_out = sum(2 if kind == "kv" else 1 for kind, _ in groups)
    return pl.pallas_call(
        functools.partial(_proj_kernel, groups=groups, rope=rope),
        grid=(bsz, t // tm),
        in_specs=[
            pl.BlockSpec((None, tm, D_MODEL), lambda b, i: (b, i, 0)),
            pl.BlockSpec((None, 1, 3 * D_MODEL), mod_map),
            pl.BlockSpec((1, D_MODEL), lambda b, i: (0, 0)),
            pl.BlockSpec(w.shape, lambda b, i: (0, 0)),
            pl.BlockSpec(aux.shape, lambda b, i: (0, 0)),
            pl.BlockSpec((tm, LANES), lambda b, i: (i, 0)),
            pl.BlockSpec((tm, LANES), lambda b, i: (i, 0)),
            pl.BlockSpec((tm, LANES), lambda b, i: (i, 0)),
        ],
        out_specs=[pl.BlockSpec((None, tm, D_MODEL), lambda b, i: (b, i, 0))] * n_out,
        out_shape=[jax.ShapeDtypeStruct((bsz, t, D_MODEL), BF16)] * n_out,
        compiler_params=_params(2),
        name=name,
    )(x, mod, norm_g.reshape(1, D_MODEL), w, aux, *tables)


def _diff_scores(q_ref, kc_ref, kx_ref, h):
    sl = slice(h * LANES, (h + 1) * LANES)
    q = q_ref[:, sl]
    lane = lax.broadcasted_iota(jnp.int32, q.shape, 1)
    zero = jnp.zeros_like(q)
    qs = jnp.concatenate([jnp.where(lane < HEAD_DIM, q, zero),
                          jnp.where(lane >= HEAD_DIM, q, zero)], axis=0)
    s_c = lax.dot_general(qs, kc_ref[:, sl], NT_DIMS, preferred_element_type=F32)
    s_x = None
    if kx_ref is not None:
        s_x = lax.dot_general(qs, kx_ref[:, sl], NT_DIMS, preferred_element_type=F32)
    return s_c, s_x


def _out_proj_residual(o_scr, w_ref, x_ref, mod_ref, fg_ref, out_ref, final):
    upd = jnp.dot(o_scr[...], w_ref[...], preferred_element_type=F32)
    x = x_ref[...] + mod_ref[:, 2 * D_MODEL:3 * D_MODEL] * upd
    if final:
        ms = jnp.mean(x * x, axis=-1, keepdims=True)
        x = x * lax.rsqrt(ms + NORM_EPS) * fg_ref[...]
    out_ref[...] = x


def _diff_attn_kernel(*refs, with_latent, lam_init, final):
    if with_latent:
        (q_ref, kc_ref, vc_ref, kx_ref, vx_ref, gate_ref, lq1_ref, lk1_ref, lq2_ref,
         lk2_ref, sg_ref, w_ref, x_ref, mod_ref, fg_ref, out_ref, o_scr) = refs
    else:
        (q_ref, kc_ref, vc_ref, gate_ref, lq1_ref, lk1_ref, lq2_ref, lk2_ref, sg_ref,
         w_ref, x_ref, mod_ref, fg_ref, out_ref, o_scr) = refs
        kx_ref = vx_ref = None
    tq = q_ref.shape[0]
    lam = (jnp.exp(jnp.sum(lq1_ref[...] * lk1_ref[...], axis=-1, keepdims=True))
           - jnp.exp(jnp.sum(lq2_ref[...] * lk2_ref[...], axis=-1, keepdims=True))
           + lam_init)
    scores = _diff_scores(q_ref, kc_ref, kx_ref, 0)
    for h in range(A_HEADS):
        sl = slice(h * LANES, (h + 1) * LANES)
        s_c, s_x = scores
        if h + 1 < A_HEADS:
            scores = _diff_scores(q_ref, kc_ref, kx_ref, h + 1)
        m = jnp.max(s_c, axis=-1, keepdims=True)
        if with_latent:
            m = jnp.maximum(m, jnp.max(s_x, axis=-1, keepdims=True))
        ones = jnp.ones((vc_ref.shape[0], LANES), BF16)
        p_c = jnp.exp2(s_c - m).astype(BF16)
        oa = jnp.dot(p_c, jnp.concatenate([vc_ref[:, sl], ones], axis=1),
                     preferred_element_type=F32)
        if with_latent:
            ones = jnp.ones((vx_ref.shape[0], LANES), BF16)
            p_x = jnp.exp2(s_x - m).astype(BF16)
            oa = oa + jnp.dot(p_x, jnp.concatenate([vx_ref[:, sl], ones], axis=1),
                              preferred_element_type=F32)
        on = oa[:, :LANES] * (1.0 / oa[:, LANES:])
        o = on[:tq] - lam * on[tq:]
        ms = jnp.mean(o * o, axis=-1, keepdims=True)
        o = o * lax.rsqrt(ms + SUBLN_EPS) * sg_ref[...] * (1.0 - lam_init)
        o_scr[:, sl] = (o * _silu(gate_ref[:, sl].astype(F32))).astype(BF16)
    _out_proj_residual(o_scr, w_ref, x_ref, mod_ref, fg_ref, out_ref, final)


def _mod_spec(mod):
    if mod.shape[0] != 1:
        return pl.BlockSpec((None, 1, 3 * D_MODEL), lambda b, i: (b, 0, 0))
    return pl.BlockSpec((None, 1, 3 * D_MODEL), lambda b, i: (0, 0, 0))


def _diff_attn_call(q, kc, vc, kx, vx, gate, lams, subln_g, lam_init, w_o, x, mod, final_g,
                    final, tq, name):
    bsz, t, _ = q.shape
    n_ctx = kc.shape[1]
    with_latent = kx is not None
    full_blk = lambda rows: pl.BlockSpec((None, rows, D_MODEL), lambda b, i: (b, 0, 0))
    tile_blk = pl.BlockSpec((None, tq, D_MODEL), lambda b, i: (b, i, 0))
    const_blk = lambda r, n: pl.BlockSpec((r, n), lambda b, i: (0, 0))
    in_specs = [tile_blk, full_blk(n_ctx), full_blk(n_ctx)]
    args = [q, kc, vc]
    if with_latent:
        in_specs += [full_blk(kx.shape[1]), full_blk(kx.shape[1])]
        args += [kx, vx]
    in_specs += ([tile_blk] + [const_blk(1, HEAD_DIM)] * 4 + [const_blk(1, 2 * HEAD_DIM)]
                 + [const_blk(D_MODEL, D_MODEL), tile_blk, _mod_spec(mod),
                    const_blk(1, D_MODEL)])
    args += ([gate] + [v.reshape(1, HEAD_DIM) for v in lams] + [subln_g.reshape(1, -1)]
             + [w_o, x, mod, final_g.reshape(1, D_MODEL)])
    return pl.pallas_call(
        functools.partial(_diff_attn_kernel, with_latent=with_latent, lam_init=lam_init,
                          final=final),
        grid=(bsz, t // tq),
        in_specs=in_specs,
        out_specs=tile_blk,
        out_shape=jax.ShapeDtypeStruct((bsz, t, D_MODEL), F32),
        scratch_shapes=[pltpu.VMEM((tq, D_MODEL), BF16)],
        compiler_params=_params(2),
        name=name,
    )(*args)


def _win_attn_kernel(sink_ref, zero_ref, q_ref, kc_ref, vc_ref, kx_ref, vx_ref, gate_ref,
                     bias_ref, w_ref, x_ref, mod_ref, fg_ref, out_ref, o_scr, s_a, s_b, *,
                     final):
    tq = Q_BLOCK
    tiles = q_ref.shape[0] // tq
    seq = kx_ref.shape[0]
    n_tiles = seq // tq
    n_ctx = kc_ref.shape[0]
    rows = B_GROUP * tq
    row1 = lax.broadcasted_iota(jnp.int32, (rows, 1), 0)
    lane_grp = lax.broadcasted_iota(jnp.int32, (tq, B_SLAB), 1) // HEAD_DIM
    units = [(t, j) for t in range(tiles) for j in range(B_KV_HEADS)]
    z = zero_ref[0]
    slots = (s_a, s_b)

    def window_start(t):
        ti = pl.program_id(1) * tiles + t
        kstart = pl.multiple_of(jnp.clip(ti * tq - WINDOW, 0, seq - BAND), WINDOW)
        return ti, kstart

    def scores(u):
        t, j = units[u]
        ti, kstart = window_start(t)
        sl = slice(j * B_SLAB, (j + 1) * B_SLAB)
        qj = q_ref[t * tq:(t + 1) * tq, sl]
        zero = jnp.zeros_like(qj)
        qs = jnp.concatenate(
            [jnp.where(lane_grp == g, qj, zero) for g in range(B_GROUP)], axis=0)
        slots[u % 2][z, :, :n_ctx] = lax.dot_general(
            qs, kc_ref[:, sl], NT_DIMS, preferred_element_type=F32)
        bias = bias_ref[jnp.where(ti == 0, 0, jnp.where(ti == n_tiles - 1, 2, 1))]
        bias = jnp.concatenate([bias] * B_GROUP, axis=0)
        slots[u % 2][z, :, n_ctx:] = lax.dot_general(
            qs, kx_ref[pl.ds(kstart, BAND), sl], NT_DIMS, preferred_element_type=F32) + bias

    scores(0)
    for u, (t, j) in enumerate(units):
        if u + 1 < len(units):
            scores(u + 1)
        _, kstart = window_start(t)
        rs = slice(t * tq, (t + 1) * tq)
        sl = slice(j * B_SLAB, (j + 1) * B_SLAB)
        sink = jnp.full((rows, 1), sink_ref[j * B_GROUP + B_GROUP - 1], F32)
        for g in range(B_GROUP - 2, -1, -1):
            sink = jnp.where(row1 < (g + 1) * tq, sink_ref[j * B_GROUP + g], sink)
        sink = sink * LOG2E
        m = jnp.maximum(jnp.max(slots[u % 2][z], axis=-1, keepdims=True), sink)
        e = jnp.exp2(slots[u % 2][z] - m)
        l = jnp.sum(e, axis=-1, keepdims=True) + jnp.exp2(sink - m)
        e = e.astype(BF16)
        o = (jnp.dot(e[:, :n_ctx], vc_ref[:, sl], preferred_element_type=F32)
             + jnp.dot(e[:, n_ctx:], vx_ref[pl.ds(kstart, BAND), sl],
                       preferred_element_type=F32))
        o = o * (1.0 / l)
        o_j = o[(B_GROUP - 1) * tq:]
        for g in range(B_GROUP - 2, -1, -1):
            o_j = jnp.where(lane_grp == g, o[g * tq:(g + 1) * tq], o_j)
        o_scr[rs, sl] = (o_j * _silu(gate_ref[rs, sl].astype(F32))).astype(BF16)
    _out_proj_residual(o_scr, w_ref, x_ref, mod_ref, fg_ref, out_ref, final)


def _band_bias(tq):
    r = jnp.arange(tq, dtype=jnp.int32)[None, :, None]
    c = jnp.arange(BAND, dtype=jnp.int32)[None, None, :]
    off = (jnp.arange(3, dtype=jnp.int32) * WINDOW)[:, None, None]
    return jnp.where(jnp.abs(c - off - r) <= WINDOW, 0.0, NEG_INF).astype(F32)


def _win_attn_call(sink, q, kc, vc, kx, vx, gate, w_o, x, mod, final_g, final):
    bsz, t, _ = q.shape
    n_ctx = kc.shape[1]
    tq = Q_BLOCK
    rows_per_step = WIN_TILES_PER_STEP * tq
    assert t // tq >= 3 and tq == WINDOW and t % rows_per_step == 0
    tile_blk = pl.BlockSpec((None, rows_per_step, D_MODEL), lambda b, i: (b, i, 0))
    full_blk = lambda rows: pl.BlockSpec((None, rows, D_MODEL), lambda b, i: (b, 0, 0))
    bias_blk = pl.BlockSpec((3, tq, BAND), lambda b, i: (0, 0, 0))
    const_blk = lambda r, n: pl.BlockSpec((r, n), lambda b, i: (0, 0))
    return pl.pallas_call(
        functools.partial(_win_attn_kernel, final=final),
        grid=(bsz, t // rows_per_step),
        in_specs=[pl.BlockSpec(memory_space=pltpu.SMEM),
                  pl.BlockSpec(memory_space=pltpu.SMEM), tile_blk, full_blk(n_ctx),
                  full_blk(n_ctx), full_blk(t), full_blk(t), tile_blk, bias_blk,
                  const_blk(D_MODEL, D_MODEL), tile_blk, _mod_spec(mod),
                  const_blk(1, D_MODEL)],
        out_specs=tile_blk,
        out_shape=jax.ShapeDtypeStruct((bsz, t, D_MODEL), F32),
        scratch_shapes=[pltpu.VMEM((rows_per_step, D_MODEL), BF16)]
        + [pltpu.VMEM((1, B_GROUP * tq, n_ctx + BAND), F32)] * 2,
        compiler_params=_params(2),
        name="win_attn",
    )(sink, jnp.zeros((1,), jnp.int32), q, kc, vc, kx, vx, gate, _band_bias(tq), w_o, x,
      mod, final_g.reshape(1, D_MODEL))


def _rope_tables(seq):
    pos = jnp.arange(seq, dtype=jnp.int32)
    row = (pos // GRID_W).astype(F32)
    col = (pos % GRID_W).astype(F32)
    axis_dim = HEAD_DIM // 2
    inv_freq = ROPE_THETA ** (-jnp.arange(0, axis_dim, 2, dtype=F32) / axis_dim)
    ang_r = row[:, None] * inv_freq
    ang_c = col[:, None] * inv_freq
    zeros = jnp.zeros_like(ang_r)
    reps = LANES // HEAD_DIM
    cos = jnp.tile(jnp.concatenate(
        [jnp.cos(ang_r), jnp.cos(ang_r), jnp.cos(ang_c), jnp.cos(ang_c)], axis=-1), (1, reps))
    sin_a = jnp.tile(jnp.concatenate(
        [-jnp.sin(ang_r), zeros, -jnp.sin(ang_c), zeros], axis=-1), (1, reps))
    sin_b = jnp.tile(jnp.concatenate(
        [zeros, jnp.sin(ang_r), zeros, jnp.sin(ang_c)], axis=-1), (1, reps))
    return cos, sin_a, sin_b


def _kv_replication_matrix():
    col = jnp.arange(D_MODEL, dtype=jnp.int32)
    src = (col // B_SLAB) * HEAD_DIM + col % HEAD_DIM
    row = jnp.arange(B_KV_HEADS * HEAD_DIM, dtype=jnp.int32)
    return (row[:, None] == src[None, :]).astype(BF16)


def kernel(x, c, ctx, c_ctx, w_mod, b_mod, norm_g, w_o, a_w_in, a_lambda_q1, a_lambda_k1,
           a_lambda_q2, a_lambda_k2, a_subln_g, b_w_in, b_sink, final_g):
    bsz, seq, _ = x.shape
    n_ctx = ctx.shape[1]
    depth = w_mod.shape[0]
    tables = _rope_tables(seq)
    ctx_rows = 2 * n_ctx
    assert bsz % 2 == 0
    ctx_tables = tuple(tb[:ctx_rows] for tb in tables)

    act = jnp.concatenate(
        [c, c_ctx[None, :], jnp.zeros((MOD_ROWS - bsz - 1, D_MODEL), F32)], axis=0)
    mods = _mod_call(act, w_mod, b_mod)

    rep = _kv_replication_matrix()
    kv_w = B_KV_HEADS * HEAD_DIM
    for i in range(depth):
        last = i == depth - 1
        j = i // 2
        mod_x = mods[i, :bsz].reshape(bsz, 1, 3 * D_MODEL)
        mod_c = mods[i, bsz:bsz + 1].reshape(1, 1, 3 * D_MODEL)

        def ctx_proj(ctx, g, w, groups, name, mod_c=mod_c):
            outs = _proj_call(ctx.reshape(-1, ctx_rows, D_MODEL), mod_c, g, w, rep,
                              ctx_tables, groups, False, ctx_rows, name)
            return [o.reshape(bsz, n_ctx, D_MODEL) for o in outs]

        w_out = w_o[i].astype(BF16)
        if i % 2 == 0:
            w = a_w_in[j].astype(BF16)
            groups = tuple((kind, n * D_MODEL) for n, kind in enumerate("qkvg"))
            qx, kx, vx, gx = _proj_call(x, mod_x, norm_g[i], w, rep, tables, groups, True,
                                        1024, "proj_a_x")
            qc, kc, vc, gc = ctx_proj(ctx, norm_g[i], w, groups, "proj_a_ctx")
            lam_init = 0.8 - 0.6 * math.exp(-0.3 * i)
            lams = (a_lambda_q1[j], a_lambda_k1[j], a_lambda_q2[j], a_lambda_k2[j])
            x = _diff_attn_call(qx, kc, vc, kx, vx, gx, lams, a_subln_g[j], lam_init, w_out,
                                x, mod_x, final_g, last, 256, "diff_attn_x")
            if not last:
                ctx = _diff_attn_call(qc, kc, vc, None, None, gc, lams, a_subln_g[j],
                                      lam_init, w_out, ctx, mod_c, final_g, False, n_ctx,
                                      "diff_attn_ctx")
        else:
            w = b_w_in[j].astype(BF16)
            groups = (("q", 0), ("kv", D_MODEL), ("g", D_MODEL + 2 * kv_w))
            qx, kx, vx, gx = _proj_call(x, mod_x, norm_g[i], w, rep, tables, groups, True,
                                        1024, "proj_b_x")
            if last:
                kc, vc = ctx_proj(ctx, norm_g[i], w, (("kv", D_MODEL),), "proj_b_ctx")
                x = _win_attn_call(b_sink[j], qx, kc, vc, kx, vx, gx, w_out, x, mod_x,
                                   final_g, True)
            else:
                raise NotImplementedError("context output of a windowed layer")
    return x
```

```python
import functools
import math

import jax
import jax.numpy as jnp
from jax import lax
from jax.experimental import pallas as pl
from jax.experimental.pallas import tpu as pltpu

D_MODEL = 1024
GRID_W = 64
HEAD_DIM = 64
LANES = 128
A_HEADS = D_MODEL // (2 * HEAD_DIM)
B_HEADS = D_MODEL // HEAD_DIM
B_KV_HEADS = 4
B_GROUP = B_HEADS // B_KV_HEADS
B_SLAB = B_GROUP * HEAD_DIM
WINDOW = 128
Q_BLOCK = 128
BAND = Q_BLOCK + 2 * WINDOW
WIN_TILES_PER_STEP = 4
ROPE_THETA = 10000.0
NORM_EPS = 1e-6
SUBLN_EPS = 1e-5
NEG_INF = -1e30
ATTN_SCALE = HEAD_DIM ** -0.5
LOG2E = math.log2(math.e)
MOD_ROWS = 40
VMEM_LIMIT = 56 * 1024 * 1024
VMEM_LIMIT_LARGE = 60 * 1024 * 1024

F32 = jnp.float32
BF16 = jnp.bfloat16
NT_DIMS = (((1,), (1,)), ((), ()))


def _params(n_grid, vmem_limit=VMEM_LIMIT):
    return pltpu.CompilerParams(
        dimension_semantics=("arbitrary",) * n_grid,
        vmem_limit_bytes=vmem_limit)


def _silu(x):
    return x * (1.0 / (1.0 + jnp.exp(-x)))


def _mod_kernel(act_ref, w_ref, b_ref, out_ref):
    a = _silu(act_ref[...])
    out_ref[...] = jnp.dot(a, w_ref[...], preferred_element_type=F32,
                           precision=lax.Precision.HIGHEST) + b_ref[...]


def _mod_call(act, w_mod, b_mod):
    depth = w_mod.shape[0]
    n_blk = w_mod.shape[2] // D_MODEL
    return pl.pallas_call(
        _mod_kernel,
        grid=(depth, n_blk),
        in_specs=[
            pl.BlockSpec((MOD_ROWS, D_MODEL), lambda l, n: (0, 0)),
            pl.BlockSpec((None, D_MODEL, D_MODEL), lambda l, n: (l, 0, n)),
            pl.BlockSpec((None, 1, D_MODEL), lambda l, n: (l, 0, n)),
        ],
        out_specs=pl.BlockSpec((None, MOD_ROWS, D_MODEL), lambda l, n: (l, 0, n)),
        out_shape=jax.ShapeDtypeStruct((depth, MOD_ROWS, 3 * D_MODEL), F32),
        compiler_params=_params(2),
        name="mod_vectors",
    )(act, w_mod, b_mod.reshape(depth, 1, 3 * D_MODEL))


def _rope_block(r, cos, sin_a, sin_b):
    return (r * cos + pltpu.roll(r, LANES - 16, 1) * sin_a
            + pltpu.roll(r, 16, 1) * sin_b)


def _proj_kernel(x_ref, mod_ref, g_ref, w_ref, aux_ref, cos_ref, sa_ref, sb_ref, *out_refs,
                 groups, rope):
    x = x_ref[...]
    ms = jnp.mean(x * x, axis=-1, keepdims=True)
    y = x * lax.rsqrt(ms + NORM_EPS) * g_ref[...]
    shift = mod_ref[:, 0:D_MODEL]
    scale = mod_ref[:, D_MODEL:2 * D_MODEL]
    h = (y * (1.0 + scale) + shift).astype(BF16)

    def rope_cols(r):
        if not rope:
            return r
        cos, sin_a, sin_b = cos_ref[...], sa_ref[...], sb_ref[...]
        return jnp.concatenate(
            [_rope_block(r[:, hb * LANES:(hb + 1) * LANES], cos, sin_a, sin_b)
             for hb in range(r.shape[1] // LANES)], axis=1)

    outs = list(out_refs)
    for kind, c0 in groups:
        if kind == "kv":
            kv_w = aux_ref.shape[0]
            r = jnp.dot(h, w_ref[:, c0:c0 + 2 * kv_w], preferred_element_type=F32)
            for part in (rope_cols(r[:, :kv_w]), r[:, kv_w:]):
                outs.pop(0)[...] = jnp.dot(part.astype(BF16), aux_ref[...],
                                           preferred_element_type=F32).astype(BF16)
            continue
        r = jnp.dot(h, w_ref[:, c0:c0 + D_MODEL], preferred_element_type=F32)
        if kind == "q":
            r = r * (ATTN_SCALE * LOG2E)
        if kind in ("q", "k"):
            r = rope_cols(r)
        outs.pop(0)[...] = r.astype(BF16)


def _proj_call(x, mod, norm_g, w, aux, tables, groups, rope, tm, name):
    bsz, t, _ = x.shape
    per_batch_mod = mod.shape[0] != 1
    mod_map = (lambda b, i: (b, 0, 0)) if per_batch_mod else (lambda b, i: (0, 0, 0))
    n_out = sum(2 if kind == "kv" else 1 for kind, _ in groups)
    return pl.pallas_call(
        functools.partial(_proj_kernel, groups=groups, rope=rope),
        grid=(bsz, t // tm),
        in_specs=[
            pl.BlockSpec((None, tm, D_MODEL), lambda b, i: (b, i, 0)),
            pl.BlockSpec((None, 1, 3 * D_MODEL), mod_map),
            pl.BlockSpec((1, D_MODEL), lambda b, i: (0, 0)),
            pl.BlockSpec(w.shape, lambda b, i: (0, 0)),
            pl.BlockSpec(aux.shape, lambda b, i: (0, 0)),
            pl.BlockSpec((tm, LANES), lambda b, i: (i, 0)),
            pl.BlockSpec((tm, LANES), lambda b, i: (i, 0)),
            pl.BlockSpec((tm, LANES), lambda b, i: (i, 0)),
        ],
        out_specs=[pl.BlockSpec((None, tm, D_MODEL), lambda b, i: (b, i, 0))] * n_out,
        out_shape=[jax.ShapeDtypeStruct((bsz, t, D_MODEL), BF16)] * n_out,
        compiler_params=_params(2),
        name=name,
    )(x, mod, norm_g.reshape(1, D_MODEL), w, aux, *tables)


def _diff_scores(q_ref, kc_ref, kx_ref, h):
    sl = slice(h * LANES, (h + 1) * LANES)
    q = q_ref[:, sl]
    lane = lax.broadcasted_iota(jnp.int32, q.shape, 1)
    zero = jnp.zeros_like(q)
    qs = jnp.concatenate([jnp.where(lane < HEAD_DIM, q, zero),
                          jnp.where(lane >= HEAD_DIM, q, zero)], axis=0)
    s_c = lax.dot_general(qs, kc_ref[:, sl], NT_DIMS, preferred_element_type=F32)
    s_x = None
    if kx_ref is not None:
        s_x = lax.dot_general(qs, kx_ref[:, sl], NT_DIMS, preferred_element_type=F32)
    return s_c, s_x


def _out_proj_residual(o_scr, w_ref, x_ref, mod_ref, fg_ref, out_ref, final):
    upd = jnp.dot(o_scr[...], w_ref[...], preferred_element_type=F32)
    x = x_ref[...] + mod_ref[:, 2 * D_MODEL:3 * D_MODEL] * upd
    if final:
        ms = jnp.mean(x * x, axis=-1, keepdims=True)
        x = x * lax.rsqrt(ms + NORM_EPS) * fg_ref[...]
    out_ref[...] = x


def _diff_attn_kernel(*refs, with_latent, lam_init, final):
    if with_latent:
        (q_ref, kc_ref, vc_ref, kx_ref, vx_ref, gate_ref, lq1_ref, lk1_ref, lq2_ref,
         lk2_ref, sg_ref, w_ref, x_ref, mod_ref, fg_ref, out_ref, o_scr) = refs
    else:
        (q_ref, kc_ref, vc_ref, gate_ref, lq1_ref, lk1_ref, lq2_ref, lk2_ref, sg_ref,
         w_ref, x_ref, mod_ref, fg_ref, out_ref, o_scr) = refs
        kx_ref = vx_ref = None
    tq = q_ref.shape[0]
    lam = (jnp.exp(jnp.sum(lq1_ref[...] * lk1_ref[...], axis=-1, keepdims=True))
           - jnp.exp(jnp.sum(lq2_ref[...] * lk2_ref[...], axis=-1, keepdims=True))
           + lam_init)
    scores = _diff_scores(q_ref, kc_ref, kx_ref, 0)
    for h in range(A_HEADS):
        sl = slice(h * LANES, (h + 1) * LANES)
        s_c, s_x = scores
        if h + 1 < A_HEADS:
            scores = _diff_scores(q_ref, kc_ref, kx_ref, h + 1)
        m = jnp.max(s_c, axis=-1, keepdims=True)
        if with_latent:
            m = jnp.maximum(m, jnp.max(s_x, axis=-1, keepdims=True))
        ones = jnp.ones((vc_ref.shape[0], LANES), BF16)
        p_c = jnp.exp2(s_c - m).astype(BF16)
        oa = jnp.dot(p_c, jnp.concatenate([vc_ref[:, sl], ones], axis=1),
                     preferred_element_type=F32)
        if with_latent:
            ones = jnp.ones((vx_ref.shape[0], LANES), BF16)
            p_x = jnp.exp2(s_x - m).astype(BF16)
            oa = oa + jnp.dot(p_x, jnp.concatenate([vx_ref[:, sl], ones], axis=1),
                              preferred_element_type=F32)
        on = oa[:, :LANES] * (1.0 / oa[:, LANES:])
        o = on[:tq] - lam * on[tq:]
        ms = jnp.mean(o * o, axis=-1, keepdims=True)
        o = o * lax.rsqrt(ms + SUBLN_EPS) * sg_ref[...] * (1.0 - lam_init)
        o_scr[:, sl] = (o * _silu(gate_ref[:, sl].astype(F32))).astype(BF16)
    _out_proj_residual(o_scr, w_ref, x_ref, mod_ref, fg_ref, out_ref, final)


def _mod_spec(mod):
    if mod.shape[0] != 1:
        return pl.BlockSpec((None, 1, 3 * D_MODEL), lambda b, i: (b, 0, 0))
    return pl.BlockSpec((None, 1, 3 * D_MODEL), lambda b, i: (0, 0, 0))


def _diff_attn_call(q, kc, vc, kx, vx, gate, lams, subln_g, lam_init, w_o, x, mod, final_g,
                    final, tq, name):
    bsz, t, _ = q.shape
    n_ctx = kc.shape[1]
    with_latent = kx is not None
    full_blk = lambda rows: pl.BlockSpec((None, rows, D_MODEL), lambda b, i: (b, 0, 0))
    tile_blk = pl.BlockSpec((None, tq, D_MODEL), lambda b, i: (b, i, 0))
    const_blk = lambda r, n: pl.BlockSpec((r, n), lambda b, i: (0, 0))
    in_specs = [tile_blk, full_blk(n_ctx), full_blk(n_ctx)]
    args = [q, kc, vc]
    if with_latent:
        in_specs += [full_blk(kx.shape[1]), full_blk(kx.shape[1])]
        args += [kx, vx]
    in_specs += ([tile_blk] + [const_blk(1, HEAD_DIM)] * 4 + [const_blk(1, 2 * HEAD_DIM)]
                 + [const_blk(D_MODEL, D_MODEL), tile_blk, _mod_spec(mod),
                    const_blk(1, D_MODEL)])
    args += ([gate] + [v.reshape(1, HEAD_DIM) for v in lams] + [subln_g.reshape(1, -1)]
             + [w_o, x, mod, final_g.reshape(1, D_MODEL)])
    return pl.pallas_call(
        functools.partial(_diff_attn_kernel, with_latent=with_latent, lam_init=lam_init,
                          final=final),
        grid=(bsz, t // tq),
        in_specs=in_specs,
        out_specs=tile_blk,
        out_shape=jax.ShapeDtypeStruct((bsz, t, D_MODEL), F32),
        scratch_shapes=[pltpu.VMEM((tq, D_MODEL), BF16)],
        compiler_params=_params(2, VMEM_LIMIT_LARGE),
        name=name,
    )(*args)


def _win_attn_kernel(sink_ref, zero_ref, q_ref, kc_ref, vc_ref, kx_ref, vx_ref, gate_ref,
                     bias_ref, w_ref, x_ref, mod_ref, fg_ref, out_ref, o_scr, s_a, s_b, *,
                     final):
    tq = Q_BLOCK
    tiles = q_ref.shape[0] // tq
    seq = kx_ref.shape[0]
    n_tiles = seq // tq
    n_ctx = kc_ref.shape[0]
    rows = B_GROUP * tq
    row1 = lax.broadcasted_iota(jnp.int32, (rows, 1), 0)
    lane_grp = lax.broadcasted_iota(jnp.int32, (tq, B_SLAB), 1) // HEAD_DIM
    units = [(t, j) for t in range(tiles) for j in range(B_KV_HEADS)]
    z = zero_ref[0]
    slots = (s_a, s_b)

    def window_start(t):
        ti = pl.program_id(1) * tiles + t
        kstart = pl.multiple_of(jnp.clip(ti * tq - WINDOW, 0, seq - BAND), WINDOW)
        return ti, kstart

    def scores(u):
        t, j = units[u]
        ti, kstart = window_start(t)
        sl = slice(j * B_SLAB, (j + 1) * B_SLAB)
        qj = q_ref[t * tq:(t + 1) * tq, sl]
        zero = jnp.zeros_like(qj)
        qs = jnp.concatenate(
            [jnp.where(lane_grp == g, qj, zero) for g in range(B_GROUP)], axis=0)
        slots[u % 2][z, :, :n_ctx] = lax.dot_general(
            qs, kc_ref[:, sl], NT_DIMS, preferred_element_type=F32)
        bias = bias_ref[jnp.where(ti == 0, 0, jnp.where(ti == n_tiles - 1, 2, 1))]
        bias = jnp.concatenate([bias] * B_GROUP, axis=0)
        slots[u % 2][z, :, n_ctx:] = lax.dot_general(
            qs, kx_ref[pl.ds(kstart, BAND), sl], NT_DIMS, preferred_element_type=F32) + bias

    scores(0)
    for u, (t, j) in enumerate(units):
        if u + 1 < len(units):
            scores(u + 1)
        _, kstart = window_start(t)
        rs = slice(t * tq, (t + 1) * tq)
        sl = slice(j * B_SLAB, (j + 1) * B_SLAB)
        sink = jnp.full((rows, 1), sink_ref[j * B_GROUP + B_GROUP - 1], F32)
        for g in range(B_GROUP - 2, -1, -1):
            sink = jnp.where(row1 < (g + 1) * tq, sink_ref[j * B_GROUP + g], sink)
        sink = sink * LOG2E
        m = jnp.maximum(jnp.max(slots[u % 2][z], axis=-1, keepdims=True), sink)
        e = jnp.exp2(slots[u % 2][z] - m)
        l = jnp.sum(e, axis=-1, keepdims=True) + jnp.exp2(sink - m)
        e = e.astype(BF16)
        o = (jnp.dot(e[:, :n_ctx], vc_ref[:, sl], preferred_element_type=F32)
             + jnp.dot(e[:, n_ctx:], vx_ref[pl.ds(kstart, BAND), sl],
                       preferred_element_type=F32))
        o = o * (1.0 / l)
        o_j = o[(B_GROUP - 1) * tq:]
        for g in range(B_GROUP - 2, -1, -1):
            o_j = jnp.where(lane_grp == g, o[g * tq:(g + 1) * tq], o_j)
        o_scr[rs, sl] = (o_j * _silu(gate_ref[rs, sl].astype(F32))).astype(BF16)
    _out_proj_residual(o_scr, w_ref, x_ref, mod_ref, fg_ref, out_ref, final)


def _band_bias(tq):
    r = jnp.arange(tq, dtype=jnp.int32)[None, :, None]
    c = jnp.arange(BAND, dtype=jnp.int32)[None, None, :]
    off = (jnp.arange(3, dtype=jnp.int32) * WINDOW)[:, None, None]
    return jnp.where(jnp.abs(c - off - r) <= WINDOW, 0.0, NEG_INF).astype(F32)


def _win_attn_call(sink, q, kc, vc, kx, vx, gate, w_o, x, mod, final_g, final):
    bsz, t, _ = q.shape
    n_ctx = kc.shape[1]
    tq = Q_BLOCK
    rows_per_step = WIN_TILES_PER_STEP * tq
    assert t // tq >= 3 and tq == WINDOW and t % rows_per_step == 0
    tile_blk = pl.BlockSpec((None, rows_per_step, D_MODEL), lambda b, i: (b, i, 0))
    full_blk = lambda rows: pl.BlockSpec((None, rows, D_MODEL), lambda b, i: (b, 0, 0))
    bias_blk = pl.BlockSpec((3, tq, BAND), lambda b, i: (0, 0, 0))
    const_blk = lambda r, n: pl.BlockSpec((r, n), lambda b, i: (0, 0))
    return pl.pallas_call(
        functools.partial(_win_attn_kernel, final=final),
        grid=(bsz, t // rows_per_step),
        in_specs=[pl.BlockSpec(memory_space=pltpu.SMEM),
                  pl.BlockSpec(memory_space=pltpu.SMEM), tile_blk, full_blk(n_ctx),
                  full_blk(n_ctx), full_blk(t), full_blk(t), tile_blk, bias_blk,
                  const_blk(D_MODEL, D_MODEL), tile_blk, _mod_spec(mod),
                  const_blk(1, D_MODEL)],
        out_specs=tile_blk,
        out_shape=jax.ShapeDtypeStruct((bsz, t, D_MODEL), F32),
        scratch_shapes=[pltpu.VMEM((rows_per_step, D_MODEL), BF16)]
        + [pltpu.VMEM((1, B_GROUP * tq, n_ctx + BAND), F32)] * 2,
        compiler_params=_params(2),
        name="win_attn",
    )(sink, jnp.zeros((1,), jnp.int32), q, kc, vc, kx, vx, gate, _band_bias(tq), w_o, x,
      mod, final_g.reshape(1, D_MODEL))


def _rope_tables(seq):
    pos = jnp.arange(seq, dtype=jnp.int32)
    row = (pos // GRID_W).astype(F32)
    col = (pos % GRID_W).astype(F32)
    axis_dim = HEAD_DIM // 2
    inv_freq = ROPE_THETA ** (-jnp.arange(0, axis_dim, 2, dtype=F32) / axis_dim)
    ang_r = row[:, None] * inv_freq
    ang_c = col[:, None] * inv_freq
    zeros = jnp.zeros_like(ang_r)
    reps = LANES // HEAD_DIM
    cos = jnp.tile(jnp.concatenate(
        [jnp.cos(ang_r), jnp.cos(ang_r), jnp.cos(ang_c), jnp.cos(ang_c)], axis=-1), (1, reps))
    sin_a = jnp.tile(jnp.concatenate(
        [-jnp.sin(ang_r), zeros, -jnp.sin(ang_c), zeros], axis=-1), (1, reps))
    sin_b = jnp.tile(jnp.concatenate(
        [zeros, jnp.sin(ang_r), zeros, jnp.sin(ang_c)], axis=-1), (1, reps))
    return cos, sin_a, sin_b


def _kv_replication_matrix():
    col = jnp.arange(D_MODEL, dtype=jnp.int32)
    src = (col // B_SLAB) * HEAD_DIM + col % HEAD_DIM
    row = jnp.arange(B_KV_HEADS * HEAD_DIM, dtype=jnp.int32)
    return (row[:, None] == src[None, :]).astype(BF16)


def kernel(x, c, ctx, c_ctx, w_mod, b_mod, norm_g, w_o, a_w_in, a_lambda_q1, a_lambda_k1,
           a_lambda_q2, a_lambda_k2, a_subln_g, b_w_in, b_sink, final_g):
    bsz, seq, _ = x.shape
    n_ctx = ctx.shape[1]
    depth = w_mod.shape[0]
    tables = _rope_tables(seq)
    ctx_rows = 2 * n_ctx
    assert bsz % 2 == 0
    ctx_tables = tuple(tb[:ctx_rows] for tb in tables)

    act = jnp.concatenate(
        [c, c_ctx[None, :], jnp.zeros((MOD_ROWS - bsz - 1, D_MODEL), F32)], axis=0)
    mods = _mod_call(act, w_mod, b_mod)

    rep = _kv_replication_matrix()
    kv_w = B_KV_HEADS * HEAD_DIM
    for i in range(depth):
        last = i == depth - 1
        j = i // 2
        mod_x = mods[i, :bsz].reshape(bsz, 1, 3 * D_MODEL)
        mod_c = mods[i, bsz:bsz + 1].reshape(1, 1, 3 * D_MODEL)

        def ctx_proj(ctx, g, w, groups, name, mod_c=mod_c):
            outs = _proj_call(ctx.reshape(-1, ctx_rows, D_MODEL), mod_c, g, w, rep,
                              ctx_tables, groups, False, ctx_rows, name)
            return [o.reshape(bsz, n_ctx, D_MODEL) for o in outs]

        w_out = w_o[i].astype(BF16)
        if i % 2 == 0:
            w = a_w_in[j].astype(BF16)
            groups = tuple((kind, n * D_MODEL) for n, kind in enumerate("qkvg"))
            qx, kx, vx, gx = _proj_call(x, mod_x, norm_g[i], w, rep, tables, groups, True,
                                        1024, "proj_a_x")
            qc, kc, vc, gc = ctx_proj(ctx, norm_g[i], w, groups, "proj_a_ctx")
            lam_init = 0.8 - 0.6 * math.exp(-0.3 * i)
            lams = (a_lambda_q1[j], a_lambda_k1[j], a_lambda_q2[j], a_lambda_k2[j])
            x = _diff_attn_call(qx, kc, vc, kx, vx, gx, lams, a_subln_g[j], lam_init, w_out,
                                x, mod_x, final_g, last, 512, "diff_attn_x")
            if not last:
                ctx = _diff_attn_call(qc, kc, vc, None, None, gc, lams, a_subln_g[j],
                                      lam_init, w_out, ctx, mod_c, final_g, False, n_ctx,
                                      "diff_attn_ctx")
        else:
            w = b_w_in[j].astype(BF16)
            groups = (("q", 0), ("kv", D_MODEL), ("g", D_MODEL + 2 * kv_w))
            qx, kx, vx, gx = _proj_call(x, mod_x, norm_g[i], w, rep, tables, groups, True,
                                        1024, "proj_b_x")
            if last:
                kc, vc = ctx_proj(ctx, norm_g[i], w, (("kv", D_MODEL),), "proj_b_ctx")
                x = _win_attn_call(b_sink[j], qx, kc, vc, kx, vx, gx, w_out, x, mod_x,
                                   final_g, True)
            else:
                raise NotImplementedError("context output of a windowed layer")
    return x
```

```python
import functools
import math

import jax
import jax.numpy as jnp
from jax import lax
from jax.experimental import pallas as pl
from jax.experimental.pallas import tpu as pltpu

D_MODEL = 1024
GRID_W = 64
HEAD_DIM = 64
LANES = 128
A_HEADS = D_MODEL // (2 * HEAD_DIM)
B_HEADS = D_MODEL // HEAD_DIM
B_KV_HEADS = 4
B_GROUP = B_HEADS // B_KV_HEADS
B_SLAB = B_GROUP * HEAD_DIM
WINDOW = 128
Q_BLOCK = 128
BAND = Q_BLOCK + 2 * WINDOW

PROJ_ROWS = 1024
DIFF_ROWS = 512
WIN_TILES_PER_STEP = 4
ROPE_THETA = 10000.0
NORM_EPS = 1e-6
SUBLN_EPS = 1e-5
NEG_INF = -1e30
ATTN_SCALE = HEAD_DIM ** -0.5
LOG2E = math.log2(math.e)
MOD_ROWS = 40
VMEM_LIMIT = 56 * 1024 * 1024
VMEM_LIMIT_LARGE = 60 * 1024 * 1024

F32 = jnp.float32
BF16 = jnp.bfloat16
NT_DIMS = (((1,), (1,)), ((), ()))


def _params(n_grid, vmem_limit=VMEM_LIMIT):
    return pltpu.CompilerParams(
        dimension_semantics=("arbitrary",) * n_grid,
        vmem_limit_bytes=vmem_limit)


def _silu(x):
    return x * (1.0 / (1.0 + jnp.exp(-x)))


def _mod_kernel(act_ref, w_ref, b_ref, out_ref):
    a = _silu(act_ref[...])
    out_ref[...] = jnp.dot(a, w_ref[...], preferred_element_type=F32,
                           precision=lax.Precision.HIGHEST) + b_ref[...]


def _mod_call(act, w_mod, b_mod):
    depth = w_mod.shape[0]
    n_blk = w_mod.shape[2] // D_MODEL
    return pl.pallas_call(
        _mod_kernel,
        grid=(depth, n_blk),
        in_specs=[
            pl.BlockSpec((MOD_ROWS, D_MODEL), lambda l, n: (0, 0)),
            pl.BlockSpec((None, D_MODEL, D_MODEL), lambda l, n: (l, 0, n)),
            pl.BlockSpec((None, 1, D_MODEL), lambda l, n: (l, 0, n)),
        ],
        out_specs=pl.BlockSpec((None, MOD_ROWS, D_MODEL), lambda l, n: (l, 0, n)),
        out_shape=jax.ShapeDtypeStruct((depth, MOD_ROWS, 3 * D_MODEL), F32),
        compiler_params=_params(2),
        name="mod_vectors",
    )(act, w_mod, b_mod.reshape(depth, 1, 3 * D_MODEL))


def _rope_block(r, cos, sin_a, sin_b):
    return (r * cos + pltpu.roll(r, LANES - 16, 1) * sin_a
            + pltpu.roll(r, 16, 1) * sin_b)


def _proj_kernel(x_ref, mod_ref, g_ref, w_ref, aux_ref, cos_ref, sa_ref, sb_ref, *out_refs,
                 groups, rope):
    x = x_ref[...]
    ms = jnp.mean(x * x, axis=-1, keepdims=True)
    y = x * lax.rsqrt(ms + NORM_EPS) * g_ref[...]
    shift = mod_ref[:, 0:D_MODEL]
    scale = mod_ref[:, D_MODEL:2 * D_MODEL]
    h = (y * (1.0 + scale) + shift).astype(BF16)

    def rope_cols(r):
        if not rope:
            return r
        cos, sin_a, sin_b = cos_ref[...], sa_ref[...], sb_ref[...]
        return jnp.concatenate(
            [_rope_block(r[:, hb * LANES:(hb + 1) * LANES], cos, sin_a, sin_b)
             for hb in range(r.shape[1] // LANES)], axis=1)

    outs = list(out_refs)
    for kind, c0 in groups:
        if kind == "kv":
            kv_w = aux_ref.shape[0]
            r = jnp.dot(h, w_ref[:, c0:c0 + 2 * kv_w], preferred_element_type=F32)
            for part in (rope_cols(r[:, :kv_w]), r[:, kv_w:]):
                outs.pop(0)[...] = jnp.dot(part.astype(BF16), aux_ref[...],
                                           preferred_element_type=F32).astype(BF16)
            continue
        r = jnp.dot(h, w_ref[:, c0:c0 + D_MODEL], preferred_element_type=F32)
        if kind == "q":
            r = r * (ATTN_SCALE * LOG2E)
        if kind in ("q", "k"):
            r = rope_cols(r)
        outs.pop(0)[...] = r.astype(BF16)


def _proj_call(x, mod, norm_g, w, aux, tables, groups, rope, tm, name):
    bsz, t, _ = x.shape
    per_batch_mod = mod.shape[0] != 1
    mod_map = (lambda b, i: (b, 0, 0)) if per_batch_mod else (lambda b, i: (0, 0, 0))
    n_out = sum(2 if kind == "kv" else 1 for kind, _ in groups)
    return pl.pallas_call(
        functools.partial(_proj_kernel, groups=groups, rope=rope),
        grid=(bsz, t // tm),
        in_specs=[
            pl.BlockSpec((None, tm, D_MODEL), lambda b, i: (b, i, 0)),
            pl.BlockSpec((None, 1, 3 * D_MODEL), mod_map),
            pl.BlockSpec((1, D_MODEL), lambda b, i: (0, 0)),
            pl.BlockSpec(w.shape, lambda b, i: (0, 0)),
            pl.BlockSpec(aux.shape, lambda b, i: (0, 0)),
            pl.BlockSpec((tm, LANES), lambda b, i: (i, 0)),
            pl.BlockSpec((tm, LANES), lambda b, i: (i, 0)),
            pl.BlockSpec((tm, LANES), lambda b, i: (i, 0)),
        ],
        out_specs=[pl.BlockSpec((None, tm, D_MODEL), lambda b, i: (b, i, 0))] * n_out,
        out_shape=[jax.ShapeDtypeStruct((bsz, t, D_MODEL), BF16)] * n_out,
        compiler_params=_params(2),
        name=name,
    )(x, mod, norm_g.reshape(1, D_MODEL), w, aux, *tables)


def _diff_scores(q_ref, kc_ref, kx_ref, h):
    sl = slice(h * LANES, (h + 1) * LANES)
    q = q_ref[:, sl]
    lane = lax.broadcasted_iota(jnp.int32, q.shape, 1)
    zero = jnp.zeros_like(q)
    qs = jnp.concatenate([jnp.where(lane < HEAD_DIM, q, zero),
                          jnp.where(lane >= HEAD_DIM, q, zero)], axis=0)
    s_c = lax.dot_general(qs, kc_ref[:, sl], NT_DIMS, preferred_element_type=F32)
    s_x = None
    if kx_ref is not None:
        s_x = lax.dot_general(qs, kx_ref[:, sl], NT_DIMS, preferred_element_type=F32)
    return s_c, s_x


def _out_proj_residual(o_scr, w_ref, x_ref, mod_ref, fg_ref, out_ref, final):
    upd = jnp.dot(o_scr[...], w_ref[...], preferred_element_type=F32)
    x = x_ref[...] + mod_ref[:, 2 * D_MODEL:3 * D_MODEL] * upd
    if final:
        ms = jnp.mean(x * x, axis=-1, keepdims=True)
        x = x * lax.rsqrt(ms + NORM_EPS) * fg_ref[...]
    out_ref[...] = x


def _diff_attn_kernel(*refs, with_latent, lam_init, final):
    if with_latent:
        (q_ref, kc_ref, vc_ref, kx_ref, vx_ref, gate_ref, lq1_ref, lk1_ref, lq2_ref,
         lk2_ref, sg_ref, w_ref, x_ref, mod_ref, fg_ref, out_ref, o_scr) = refs
    else:
        (q_ref, kc_ref, vc_ref, gate_ref, lq1_ref, lk1_ref, lq2_ref, lk2_ref, sg_ref,
         w_ref, x_ref, mod_ref, fg_ref, out_ref, o_scr) = refs
        kx_ref = vx_ref = None
    tq = q_ref.shape[0]
    lam = (jnp.exp(jnp.sum(lq1_ref[...] * lk1_ref[...], axis=-1, keepdims=True))
           - jnp.exp(jnp.sum(lq2_ref[...] * lk2_ref[...], axis=-1, keepdims=True))
           + lam_init)
    scores = _diff_scores(q_ref, kc_ref, kx_ref, 0)
    for h in range(A_HEADS):
        sl = slice(h * LANES, (h + 1) * LANES)
        s_c, s_x = scores
        if h + 1 < A_HEADS:
            scores = _diff_scores(q_ref, kc_ref, kx_ref, h + 1)
        m = jnp.max(s_c, axis=-1, keepdims=True)
        if with_latent:
            m = jnp.maximum(m, jnp.max(s_x, axis=-1, keepdims=True))
        ones = jnp.ones((vc_ref.shape[0], LANES), BF16)
        p_c = jnp.exp2(s_c - m).astype(BF16)
        oa = jnp.dot(p_c, jnp.concatenate([vc_ref[:, sl], ones], axis=1),
                     preferred_element_type=F32)
        if with_latent:
            ones = jnp.ones((vx_ref.shape[0], LANES), BF16)
            p_x = jnp.exp2(s_x - m).astype(BF16)
            oa = oa + jnp.dot(p_x, jnp.concatenate([vx_ref[:, sl], ones], axis=1),
                              preferred_element_type=F32)
        on = oa[:, :LANES] * (1.0 / oa[:, LANES:])
        o = on[:tq] - lam * on[tq:]
        ms = jnp.mean(o * o, axis=-1, keepdims=True)
        o = o * lax.rsqrt(ms + SUBLN_EPS) * sg_ref[...] * (1.0 - lam_init)
        o_scr[:, sl] = (o * _silu(gate_ref[:, sl].astype(F32))).astype(BF16)
    _out_proj_residual(o_scr, w_ref, x_ref, mod_ref, fg_ref, out_ref, final)


def _mod_spec(mod):
    if mod.shape[0] != 1:
        return pl.BlockSpec((None, 1, 3 * D_MODEL), lambda b, i: (b, 0, 0))
    return pl.BlockSpec((None, 1, 3 * D_MODEL), lambda b, i: (0, 0, 0))


def _diff_attn_call(q, kc, vc, kx, vx, gate, lams, subln_g, lam_init, w_o, x, mod, final_g,
                    final, tq, name):
    bsz, t, _ = q.shape
    n_ctx = kc.shape[1]
    with_latent = kx is not None
    full_blk = lambda rows: pl.BlockSpec((None, rows, D_MODEL), lambda b, i: (b, 0, 0))
    tile_blk = pl.BlockSpec((None, tq, D_MODEL), lambda b, i: (b, i, 0))
    const_blk = lambda r, n: pl.BlockSpec((r, n), lambda b, i: (0, 0))
    in_specs = [tile_blk, full_blk(n_ctx), full_blk(n_ctx)]
    args = [q, kc, vc]
    if with_latent:
        in_specs += [full_blk(kx.shape[1]), full_blk(kx.shape[1])]
        args += [kx, vx]
    in_specs += ([tile_blk] + [const_blk(1, HEAD_DIM)] * 4 + [const_blk(1, 2 * HEAD_DIM)]
                 + [const_blk(D_MODEL, D_MODEL), tile_blk, _mod_spec(mod),
                    const_blk(1, D_MODEL)])
    args += ([gate] + [v.reshape(1, HEAD_DIM) for v in lams] + [subln_g.reshape(1, -1)]
             + [w_o, x, mod, final_g.reshape(1, D_MODEL)])
    return pl.pallas_call(
        functools.partial(_diff_attn_kernel, with_latent=with_latent, lam_init=lam_init,
                          final=final),
        grid=(bsz, t // tq),
        in_specs=in_specs,
        out_specs=tile_blk,
        out_shape=jax.ShapeDtypeStruct((bsz, t, D_MODEL), F32),
        scratch_shapes=[pltpu.VMEM((tq, D_MODEL), BF16)],
        compiler_params=_params(2, VMEM_LIMIT_LARGE),
        name=name,
    )(*args)


def _win_attn_kernel(sink_ref, zero_ref, q_ref, kc_ref, vc_ref, kx_ref, vx_ref, gate_ref,
                     bias_ref, w_ref, x_ref, mod_ref, fg_ref, out_ref, o_scr, s_a, s_b, *,
                     final):
    tq = Q_BLOCK
    tiles = q_ref.shape[0] // tq
    seq = kx_ref.shape[0]
    n_tiles = seq // tq
    n_ctx = kc_ref.shape[0]
    rows = B_GROUP * tq
    row1 = lax.broadcasted_iota(jnp.int32, (rows, 1), 0)
    lane_grp = lax.broadcasted_iota(jnp.int32, (tq, B_SLAB), 1) // HEAD_DIM
    units = [(t, j) for t in range(tiles) for j in range(B_KV_HEADS)]
    z = zero_ref[0]
    slots = (s_a, s_b)

    def window_start(t):
        ti = pl.program_id(1) * tiles + t
        kstart = pl.multiple_of(jnp.clip(ti * tq - WINDOW, 0, seq - BAND), WINDOW)
        return ti, kstart

    def scores(u):
        t, j = units[u]
        ti, kstart = window_start(t)
        sl = slice(j * B_SLAB, (j + 1) * B_SLAB)
        qj = q_ref[t * tq:(t + 1) * tq, sl]
        zero = jnp.zeros_like(qj)
        qs = jnp.concatenate(
            [jnp.where(lane_grp == g, qj, zero) for g in range(B_GROUP)], axis=0)
        slots[u % 2][z, :, :n_ctx] = lax.dot_general(
            qs, kc_ref[:, sl], NT_DIMS, preferred_element_type=F32)
        bias = bias_ref[jnp.where(ti == 0, 0, jnp.where(ti == n_tiles - 1, 2, 1))]
        bias = jnp.concatenate([bias] * B_GROUP, axis=0)
        slots[u % 2][z, :, n_ctx:] = lax.dot_general(
            qs, kx_ref[pl.ds(kstart, BAND), sl], NT_DIMS, preferred_element_type=F32) + bias

    scores(0)
    for u, (t, j) in enumerate(units):
        if u + 1 < len(units):
            scores(u + 1)
        _, kstart = window_start(t)
        rs = slice(t * tq, (t + 1) * tq)
        sl = slice(j * B_SLAB, (j + 1) * B_SLAB)
        sink = jnp.full((rows, 1), sink_ref[j * B_GROUP + B_GROUP - 1], F32)
        for g in range(B_GROUP - 2, -1, -1):
            sink = jnp.where(row1 < (g + 1) * tq, sink_ref[j * B_GROUP + g], sink)
        sink = sink * LOG2E
        m = jnp.maximum(jnp.max(slots[u % 2][z], axis=-1, keepdims=True), sink)
        e = jnp.exp2(slots[u % 2][z] - m)
        l = jnp.sum(e, axis=-1, keepdims=True) + jnp.exp2(sink - m)
        e = e.astype(BF16)
        o = (jnp.dot(e[:, :n_ctx], vc_ref[:, sl], preferred_element_type=F32)
             + jnp.dot(e[:, n_ctx:], vx_ref[pl.ds(kstart, BAND), sl],
                       preferred_element_type=F32))
        o = o * (1.0 / l)
        o_j = o[(B_GROUP - 1) * tq:]
        for g in range(B_GROUP - 2, -1, -1):
            o_j = jnp.where(lane_grp == g, o[g * tq:(g + 1) * tq], o_j)
        o_scr[rs, sl] = (o_j * _silu(gate_ref[rs, sl].astype(F32))).astype(BF16)
    _out_proj_residual(o_scr, w_ref, x_ref, mod_ref, fg_ref, out_ref, final)


def _band_bias(tq):
    r = jnp.arange(tq, dtype=jnp.int32)[None, :, None]
    c = jnp.arange(BAND, dtype=jnp.int32)[None, None, :]
    off = (jnp.arange(3, dtype=jnp.int32) * WINDOW)[:, None, None]
    return jnp.where(jnp.abs(c - off - r) <= WINDOW, 0.0, NEG_INF).astype(F32)


def _win_attn_call(sink, q, kc, vc, kx, vx, gate, w_o, x, mod, final_g, final):
    bsz, t, _ = q.shape
    n_ctx = kc.shape[1]
    tq = Q_BLOCK
    rows_per_step = WIN_TILES_PER_STEP * tq
    assert t // tq >= 3 and tq == WINDOW and t % rows_per_step == 0
    tile_blk = pl.BlockSpec((None, rows_per_step, D_MODEL), lambda b, i: (b, i, 0))
    full_blk = lambda rows: pl.BlockSpec((None, rows, D_MODEL), lambda b, i: (b, 0, 0))
    bias_blk = pl.BlockSpec((3, tq, BAND), lambda b, i: (0, 0, 0))
    const_blk = lambda r, n: pl.BlockSpec((r, n), lambda b, i: (0, 0))
    return pl.pallas_call(
        functools.partial(_win_attn_kernel, final=final),
        grid=(bsz, t // rows_per_step),
        in_specs=[pl.BlockSpec(memory_space=pltpu.SMEM),
                  pl.BlockSpec(memory_space=pltpu.SMEM), tile_blk, full_blk(n_ctx),
                  full_blk(n_ctx), full_blk(t), full_blk(t), tile_blk, bias_blk,
                  const_blk(D_MODEL, D_MODEL), tile_blk, _mod_spec(mod),
                  const_blk(1, D_MODEL)],
        out_specs=tile_blk,
        out_shape=jax.ShapeDtypeStruct((bsz, t, D_MODEL), F32),
        scratch_shapes=[pltpu.VMEM((rows_per_step, D_MODEL), BF16)]
        + [pltpu.VMEM((1, B_GROUP * tq, n_ctx + BAND), F32)] * 2,
        compiler_params=_params(2),
        name="win_attn",
    )(sink, jnp.zeros((1,), jnp.int32), q, kc, vc, kx, vx, gate, _band_bias(tq), w_o, x,
      mod, final_g.reshape(1, D_MODEL))


def _rope_tables(seq):
    pos = jnp.arange(seq, dtype=jnp.int32)
    row = (pos // GRID_W).astype(F32)
    col = (pos % GRID_W).astype(F32)
    axis_dim = HEAD_DIM // 2
    inv_freq = ROPE_THETA ** (-jnp.arange(0, axis_dim, 2, dtype=F32) / axis_dim)
    ang_r = row[:, None] * inv_freq
    ang_c = col[:, None] * inv_freq
    zeros = jnp.zeros_like(ang_r)
    reps = LANES // HEAD_DIM
    cos = jnp.tile(jnp.concatenate(
        [jnp.cos(ang_r), jnp.cos(ang_r), jnp.cos(ang_c), jnp.cos(ang_c)], axis=-1), (1, reps))
    sin_a = jnp.tile(jnp.concatenate(
        [-jnp.sin(ang_r), zeros, -jnp.sin(ang_c), zeros], axis=-1), (1, reps))
    sin_b = jnp.tile(jnp.concatenate(
        [zeros, jnp.sin(ang_r), zeros, jnp.sin(ang_c)], axis=-1), (1, reps))
    return cos, sin_a, sin_b


def _kv_replication_matrix():
    col = jnp.arange(D_MODEL, dtype=jnp.int32)
    src = (col // B_SLAB) * HEAD_DIM + col % HEAD_DIM
    row = jnp.arange(B_KV_HEADS * HEAD_DIM, dtype=jnp.int32)
    return (row[:, None] == src[None, :]).astype(BF16)


def kernel(x, c, ctx, c_ctx, w_mod, b_mod, norm_g, w_o, a_w_in, a_lambda_q1, a_lambda_k1,
           a_lambda_q2, a_lambda_k2, a_subln_g, b_w_in, b_sink, final_g):
    bsz, seq, _ = x.shape
    n_ctx = ctx.shape[1]
    depth = w_mod.shape[0]
    tables = _rope_tables(seq)
    ctx_rows = PROJ_ROWS
    assert (bsz * n_ctx) % ctx_rows == 0 and ctx_rows % n_ctx == 0 and seq >= ctx_rows
    ctx_tables = tuple(tb[:ctx_rows] for tb in tables)

    act = jnp.concatenate(
        [c, c_ctx[None, :], jnp.zeros((MOD_ROWS - bsz - 1, D_MODEL), F32)], axis=0)
    mods = _mod_call(act, w_mod, b_mod)

    rep = _kv_replication_matrix()
    kv_w = B_KV_HEADS * HEAD_DIM
    for i in range(depth):
        last = i == depth - 1
        j = i // 2
        mod_x = mods[i, :bsz].reshape(bsz, 1, 3 * D_MODEL)
        mod_c = mods[i, bsz:bsz + 1].reshape(1, 1, 3 * D_MODEL)

        def ctx_proj(ctx, g, w, groups, name, mod_c=mod_c):
            outs = _proj_call(ctx.reshape(-1, ctx_rows, D_MODEL), mod_c, g, w, rep,
                              ctx_tables, groups, False, ctx_rows, name)
            return [o.reshape(bsz, n_ctx, D_MODEL) for o in outs]

        w_out = w_o[i].astype(BF16)
        if i % 2 == 0:
            w = a_w_in[j].astype(BF16)
            groups = tuple((kind, n * D_MODEL) for n, kind in enumerate("qkvg"))
            qx, kx, vx, gx = _proj_call(x, mod_x, norm_g[i], w, rep, tables, groups, True,
                                        PROJ_ROWS, "proj_a_x")
            qc, kc, vc, gc = ctx_proj(ctx, norm_g[i], w, groups, "proj_a_ctx")
            lam_init = 0.8 - 0.6 * math.exp(-0.3 * i)
            lams = (a_lambda_q1[j], a_lambda_k1[j], a_lambda_q2[j], a_lambda_k2[j])
            x = _diff_attn_call(qx, kc, vc, kx, vx, gx, lams, a_subln_g[j], lam_init, w_out,
                                x, mod_x, final_g, last, DIFF_ROWS, "diff_attn_x")
            if not last:
                ctx = _diff_attn_call(qc, kc, vc, None, None, gc, lams, a_subln_g[j],
                                      lam_init, w_out, ctx, mod_c, final_g, False, n_ctx,
                                      "diff_attn_ctx")
        else:
            w = b_w_in[j].astype(BF16)
            groups = (("q", 0), ("kv", D_MODEL), ("g", D_MODEL + 2 * kv_w))
            qx, kx, vx, gx = _proj_call(x, mod_x, norm_g[i], w, rep, tables, groups, True,
                                        PROJ_ROWS, "proj_b_x")
            if last:
                kc, vc = ctx_proj(ctx, norm_g[i], w, (("kv", D_MODEL),), "proj_b_ctx")
                x = _win_attn_call(b_sink[j], qx, kc, vc, kx, vx, gx, w_out, x, mod_x,
                                   final_g, True)
            else:
                raise NotImplementedError("context output of a windowed layer")
    return x
```

```python
import functools
import math

import jax
import jax.numpy as jnp
from jax import lax
from jax.experimental import pallas as pl
from jax.experimental.pallas import tpu as pltpu

D_MODEL = 1024
GRID_W = 64
HEAD_DIM = 64
LANES = 128
A_HEADS = D_MODEL // (2 * HEAD_DIM)
B_HEADS = D_MODEL // HEAD_DIM
B_KV_HEADS = 4
B_GROUP = B_HEADS // B_KV_HEADS
B_SLAB = B_GROUP * HEAD_DIM
WINDOW = 128
Q_BLOCK = 128
BAND = Q_BLOCK + 2 * WINDOW

PROJ_ROWS = 1024
DIFF_ROWS = 512
WIN_TILES_PER_STEP = 4
ROPE_THETA = 10000.0
NORM_EPS = 1e-6
SUBLN_EPS = 1e-5
NEG_INF = -1e30
ATTN_SCALE = HEAD_DIM ** -0.5
LOG2E = math.log2(math.e)
MOD_ROWS = 40
VMEM_LIMIT = 56 * 1024 * 1024
VMEM_LIMIT_LARGE = 60 * 1024 * 1024

F32 = jnp.float32
BF16 = jnp.bfloat16
NT_DIMS = (((1,), (1,)), ((), ()))


def _params(n_grid, vmem_limit=VMEM_LIMIT):
    return pltpu.CompilerParams(
        dimension_semantics=("arbitrary",) * n_grid,
        vmem_limit_bytes=vmem_limit)


def _silu(x):
    return x * (1.0 / (1.0 + jnp.exp(-x)))


def _mod_kernel(act_ref, w_ref, b_ref, out_ref):
    a = _silu(act_ref[...])
    out_ref[...] = jnp.dot(a, w_ref[...], preferred_element_type=F32,
                           precision=lax.Precision.HIGHEST) + b_ref[...]


def _mod_call(act, w_mod, b_mod):
    depth = w_mod.shape[0]
    n_blk = w_mod.shape[2] // D_MODEL
    return pl.pallas_call(
        _mod_kernel,
        grid=(depth, n_blk),
        in_specs=[
            pl.BlockSpec((MOD_ROWS, D_MODEL), lambda l, n: (0, 0)),
            pl.BlockSpec((None, D_MODEL, D_MODEL), lambda l, n: (l, 0, n)),
            pl.BlockSpec((None, 1, D_MODEL), lambda l, n: (l, 0, n)),
        ],
        out_specs=pl.BlockSpec((None, MOD_ROWS, D_MODEL), lambda l, n: (l, 0, n)),
        out_shape=jax.ShapeDtypeStruct((depth, MOD_ROWS, 3 * D_MODEL), F32),
        compiler_params=_params(2),
        name="mod_vectors",
    )(act, w_mod, b_mod.reshape(depth, 1, 3 * D_MODEL))


def _rope_block(r, cos, sin_a, sin_b):
    return (r * cos + pltpu.roll(r, LANES - 16, 1) * sin_a
            + pltpu.roll(r, 16, 1) * sin_b)


def _replicate_kv_lanes(r):
    lane = lax.broadcasted_iota(jnp.int32, (r.shape[0], LANES), 1)
    slabs = []
    for p in range(r.shape[1] // LANES):
        v = r[:, p * LANES:(p + 1) * LANES]
        swapped = pltpu.roll(v, HEAD_DIM, 1)
        first = jnp.where(lane < HEAD_DIM, v, swapped)
        second = jnp.where(lane < HEAD_DIM, swapped, v)
        slabs += [first] * (B_GROUP // 2) + [second] * (B_GROUP // 2)
    return jnp.concatenate(slabs, axis=1)


def _proj_kernel(x_ref, mod_ref, g_ref, w_ref, cos_ref, sa_ref, sb_ref, *out_refs,
                 groups, rope):
    x = x_ref[...]
    ms = jnp.mean(x * x, axis=-1, keepdims=True)
    y = x * lax.rsqrt(ms + NORM_EPS) * g_ref[...]
    shift = mod_ref[:, 0:D_MODEL]
    scale = mod_ref[:, D_MODEL:2 * D_MODEL]
    h = (y * (1.0 + scale) + shift).astype(BF16)

    def rope_cols(r):
        if not rope:
            return r
        cos, sin_a, sin_b = cos_ref[...], sa_ref[...], sb_ref[...]
        return jnp.concatenate(
            [_rope_block(r[:, hb * LANES:(hb + 1) * LANES], cos, sin_a, sin_b)
             for hb in range(r.shape[1] // LANES)], axis=1)

    outs = list(out_refs)
    for kind, c0 in groups:
        if kind == "kv":
            kv_w = B_KV_HEADS * HEAD_DIM
            r = jnp.dot(h, w_ref[:, c0:c0 + 2 * kv_w], preferred_element_type=F32)
            for part in (rope_cols(r[:, :kv_w]), r[:, kv_w:]):
                outs.pop(0)[...] = _replicate_kv_lanes(part).astype(BF16)
            continue
        r = jnp.dot(h, w_ref[:, c0:c0 + D_MODEL], preferred_element_type=F32)
        if kind == "q":
            r = r * (ATTN_SCALE * LOG2E)
        if kind in ("q", "k"):
            r = rope_cols(r)
        outs.pop(0)[...] = r.astype(BF16)


def _proj_call(x, mod, norm_g, w, tables, groups, rope, tm, name):
    bsz, t, _ = x.shape
    per_batch_mod = mod.shape[0] != 1
    mod_map = (lambda b, i: (b, 0, 0)) if per_batch_mod else (lambda b, i: (0, 0, 0))
    n_out = sum(2 if kind == "kv" else 1 for kind, _ in groups)
    return pl.pallas_call(
        functools.partial(_proj_kernel, groups=groups, rope=rope),
        grid=(bsz, t // tm),
        in_specs=[
            pl.BlockSpec((None, tm, D_MODEL), lambda b, i: (b, i, 0)),
            pl.BlockSpec((None, 1, 3 * D_MODEL), mod_map),
            pl.BlockSpec((1, D_MODEL), lambda b, i: (0, 0)),
            pl.BlockSpec(w.shape, lambda b, i: (0, 0)),
            pl.BlockSpec((tm, LANES), lambda b, i: (i, 0)),
            pl.BlockSpec((tm, LANES), lambda b, i: (i, 0)),
            pl.BlockSpec((tm, LANES), lambda b, i: (i, 0)),
        ],
        out_specs=[pl.BlockSpec((None, tm, D_MODEL), lambda b, i: (b, i, 0))] * n_out,
        out_shape=[jax.ShapeDtypeStruct((bsz, t, D_MODEL), BF16)] * n_out,
        compiler_params=_params(2),
        name=name,
    )(x, mod, norm_g.reshape(1, D_MODEL), w, *tables)


def _diff_scores(q_ref, kc_ref, kx_ref, h):
    sl = slice(h * LANES, (h + 1) * LANES)
    q = q_ref[:, sl]
    lane = lax.broadcasted_iota(jnp.int32, q.shape, 1)
    zero = jnp.zeros_like(q)
    qs = jnp.concatenate([jnp.where(lane < HEAD_DIM, q, zero),
                          jnp.where(lane >= HEAD_DIM, q, zero)], axis=0)
    s_c = lax.dot_general(qs, kc_ref[:, sl], NT_DIMS, preferred_element_type=F32)
    s_x = None
    if kx_ref is not None:
        s_x = lax.dot_general(qs, kx_ref[:, sl], NT_DIMS, preferred_element_type=F32)
    return s_c, s_x


def _out_proj_residual(o_scr, w_ref, x_ref, mod_ref, fg_ref, out_ref, final):
    upd = jnp.dot(o_scr[...], w_ref[...], preferred_element_type=F32)
    x = x_ref[...] + mod_ref[:, 2 * D_MODEL:3 * D_MODEL] * upd
    if final:
        ms = jnp.mean(x * x, axis=-1, keepdims=True)
        x = x * lax.rsqrt(ms + NORM_EPS) * fg_ref[...]
    out_ref[...] = x


def _diff_attn_kernel(*refs, with_latent, lam_init, final):
    if with_latent:
        (q_ref, kc_ref, vc_ref, kx_ref, vx_ref, gate_ref, lq1_ref, lk1_ref, lq2_ref,
         lk2_ref, sg_ref, w_ref, x_ref, mod_ref, fg_ref, out_ref, o_scr) = refs
    else:
        (q_ref, kc_ref, vc_ref, gate_ref, lq1_ref, lk1_ref, lq2_ref, lk2_ref, sg_ref,
         w_ref, x_ref, mod_ref, fg_ref, out_ref, o_scr) = refs
        kx_ref = vx_ref = None
    tq = q_ref.shape[0]
    lam = (jnp.exp(jnp.sum(lq1_ref[...] * lk1_ref[...], axis=-1, keepdims=True))
           - jnp.exp(jnp.sum(lq2_ref[...] * lk2_ref[...], axis=-1, keepdims=True))
           + lam_init)
    scores = _diff_scores(q_ref, kc_ref, kx_ref, 0)
    for h in range(A_HEADS):
        sl = slice(h * LANES, (h + 1) * LANES)
        s_c, s_x = scores
        if h + 1 < A_HEADS:
            scores = _diff_scores(q_ref, kc_ref, kx_ref, h + 1)
        m = jnp.max(s_c, axis=-1, keepdims=True)
        if with_latent:
            m = jnp.maximum(m, jnp.max(s_x, axis=-1, keepdims=True))
        ones = jnp.ones((vc_ref.shape[0], LANES), BF16)
        p_c = jnp.exp2(s_c - m).astype(BF16)
        oa = jnp.dot(p_c, jnp.concatenate([vc_ref[:, sl], ones], axis=1),
                     preferred_element_type=F32)
        if with_latent:
            ones = jnp.ones((vx_ref.shape[0], LANES), BF16)
            p_x = jnp.exp2(s_x - m).astype(BF16)
            oa = oa + jnp.dot(p_x, jnp.concatenate([vx_ref[:, sl], ones], axis=1),
                              preferred_element_type=F32)
        on = oa[:, :LANES] * (1.0 / oa[:, LANES:])
        o = on[:tq] - lam * on[tq:]
        ms = jnp.mean(o * o, axis=-1, keepdims=True)
        o = o * lax.rsqrt(ms + SUBLN_EPS) * sg_ref[...] * (1.0 - lam_init)
        o_scr[:, sl] = (o * _silu(gate_ref[:, sl].astype(F32))).astype(BF16)
    _out_proj_residual(o_scr, w_ref, x_ref, mod_ref, fg_ref, out_ref, final)


def _mod_spec(mod):
    if mod.shape[0] != 1:
        return pl.BlockSpec((None, 1, 3 * D_MODEL), lambda b, i: (b, 0, 0))
    return pl.BlockSpec((None, 1, 3 * D_MODEL), lambda b, i: (0, 0, 0))


def _diff_attn_call(q, kc, vc, kx, vx, gate, lams, subln_g, lam_init, w_o, x, mod, final_g,
                    final, tq, name):
    bsz, t, _ = q.shape
    n_ctx = kc.shape[1]
    with_latent = kx is not None
    full_blk = lambda rows: pl.BlockSpec((None, rows, D_MODEL), lambda b, i: (b, 0, 0))
    tile_blk = pl.BlockSpec((None, tq, D_MODEL), lambda b, i: (b, i, 0))
    const_blk = lambda r, n: pl.BlockSpec((r, n), lambda b, i: (0, 0))
    in_specs = [tile_blk, full_blk(n_ctx), full_blk(n_ctx)]
    args = [q, kc, vc]
    if with_latent:
        in_specs += [full_blk(kx.shape[1]), full_blk(kx.shape[1])]
        args += [kx, vx]
    in_specs += ([tile_blk] + [const_blk(1, HEAD_DIM)] * 4 + [const_blk(1, 2 * HEAD_DIM)]
                 + [const_blk(D_MODEL, D_MODEL), tile_blk, _mod_spec(mod),
                    const_blk(1, D_MODEL)])
    args += ([gate] + [v.reshape(1, HEAD_DIM) for v in lams] + [subln_g.reshape(1, -1)]
             + [w_o, x, mod, final_g.reshape(1, D_MODEL)])
    return pl.pallas_call(
        functools.partial(_diff_attn_kernel, with_latent=with_latent, lam_init=lam_init,
                          final=final),
        grid=(bsz, t // tq),
        in_specs=in_specs,
        out_specs=tile_blk,
        out_shape=jax.ShapeDtypeStruct((bsz, t, D_MODEL), F32),
        scratch_shapes=[pltpu.VMEM((tq, D_MODEL), BF16)],
        compiler_params=_params(2, VMEM_LIMIT_LARGE),
        name=name,
    )(*args)


def _win_attn_kernel(sink_ref, zero_ref, q_ref, kc_ref, vc_ref, kx_ref, vx_ref, gate_ref,
                     bias_ref, w_ref, x_ref, mod_ref, fg_ref, out_ref, o_scr, s_a, s_b, *,
                     final):
    tq = Q_BLOCK
    tiles = q_ref.shape[0] // tq
    seq = kx_ref.shape[0]
    n_tiles = seq // tq
    n_ctx = kc_ref.shape[0]
    rows = B_GROUP * tq
    row1 = lax.broadcasted_iota(jnp.int32, (rows, 1), 0)
    lane_grp = lax.broadcasted_iota(jnp.int32, (tq, B_SLAB), 1) // HEAD_DIM
    units = [(t, j) for t in range(tiles) for j in range(B_KV_HEADS)]
    z = zero_ref[0]
    slots = (s_a, s_b)

    def window_start(t):
        ti = pl.program_id(1) * tiles + t
        kstart = pl.multiple_of(jnp.clip(ti * tq - WINDOW, 0, seq - BAND), WINDOW)
        return ti, kstart

    def scores(u):
        t, j = units[u]
        ti, kstart = window_start(t)
        sl = slice(j * B_SLAB, (j + 1) * B_SLAB)
        qj = q_ref[t * tq:(t + 1) * tq, sl]
        zero = jnp.zeros_like(qj)
        qs = jnp.concatenate(
            [jnp.where(lane_grp == g, qj, zero) for g in range(B_GROUP)], axis=0)
        slots[u % 2][z, :, :n_ctx] = lax.dot_general(
            qs, kc_ref[:, sl], NT_DIMS, preferred_element_type=F32)
        bias = bias_ref[jnp.where(ti == 0, 0, jnp.where(ti == n_tiles - 1, 2, 1))]
        bias = jnp.concatenate([bias] * B_GROUP, axis=0)
        slots[u % 2][z, :, n_ctx:] = lax.dot_general(
            qs, kx_ref[pl.ds(kstart, BAND), sl], NT_DIMS, preferred_element_type=F32) + bias

    scores(0)
    for u, (t, j) in enumerate(units):
        if u + 1 < len(units):
            scores(u + 1)
        _, kstart = window_start(t)
        rs = slice(t * tq, (t + 1) * tq)
        sl = slice(j * B_SLAB, (j + 1) * B_SLAB)
        sink = jnp.full((rows, 1), sink_ref[j * B_GROUP + B_GROUP - 1], F32)
        for g in range(B_GROUP - 2, -1, -1):
            sink = jnp.where(row1 < (g + 1) * tq, sink_ref[j * B_GROUP + g], sink)
        sink = sink * LOG2E
        m = jnp.maximum(jnp.max(slots[u % 2][z], axis=-1, keepdims=True), sink)
        e = jnp.exp2(slots[u % 2][z] - m)
        l = jnp.sum(e, axis=-1, keepdims=True) + jnp.exp2(sink - m)
        e = e.astype(BF16)
        o = (jnp.dot(e[:, :n_ctx], vc_ref[:, sl], preferred_element_type=F32)
             + jnp.dot(e[:, n_ctx:], vx_ref[pl.ds(kstart, BAND), sl],
                       preferred_element_type=F32))
        o = o * (1.0 / l)
        o_j = o[(B_GROUP - 1) * tq:]
        for g in range(B_GROUP - 2, -1, -1):
            o_j = jnp.where(lane_grp == g, o[g * tq:(g + 1) * tq], o_j)
        o_scr[rs, sl] = (o_j * _silu(gate_ref[rs, sl].astype(F32))).astype(BF16)
    _out_proj_residual(o_scr, w_ref, x_ref, mod_ref, fg_ref, out_ref, final)


def _band_bias(tq):
    r = jnp.arange(tq, dtype=jnp.int32)[None, :, None]
    c = jnp.arange(BAND, dtype=jnp.int32)[None, None, :]
    off = (jnp.arange(3, dtype=jnp.int32) * WINDOW)[:, None, None]
    return jnp.where(jnp.abs(c - off - r) <= WINDOW, 0.0, NEG_INF).astype(F32)


def _win_attn_call(sink, q, kc, vc, kx, vx, gate, w_o, x, mod, final_g, final):
    bsz, t, _ = q.shape
    n_ctx = kc.shape[1]
    tq = Q_BLOCK
    rows_per_step = WIN_TILES_PER_STEP * tq
    assert t // tq >= 3 and tq == WINDOW and t % rows_per_step == 0
    tile_blk = pl.BlockSpec((None, rows_per_step, D_MODEL), lambda b, i: (b, i, 0))
    full_blk = lambda rows: pl.BlockSpec((None, rows, D_MODEL), lambda b, i: (b, 0, 0))
    bias_blk = pl.BlockSpec((3, tq, BAND), lambda b, i: (0, 0, 0))
    const_blk = lambda r, n: pl.BlockSpec((r, n), lambda b, i: (0, 0))
    return pl.pallas_call(
        functools.partial(_win_attn_kernel, final=final),
        grid=(bsz, t // rows_per_step),
        in_specs=[pl.BlockSpec(memory_space=pltpu.SMEM),
                  pl.BlockSpec(memory_space=pltpu.SMEM), tile_blk, full_blk(n_ctx),
                  full_blk(n_ctx), full_blk(t), full_blk(t), tile_blk, bias_blk,
                  const_blk(D_MODEL, D_MODEL), tile_blk, _mod_spec(mod),
                  const_blk(1, D_MODEL)],
        out_specs=tile_blk,
        out_shape=jax.ShapeDtypeStruct((bsz, t, D_MODEL), F32),
        scratch_shapes=[pltpu.VMEM((rows_per_step, D_MODEL), BF16)]
        + [pltpu.VMEM((1, B_GROUP * tq, n_ctx + BAND), F32)] * 2,
        compiler_params=_params(2),
        name="win_attn",
    )(sink, jnp.zeros((1,), jnp.int32), q, kc, vc, kx, vx, gate, _band_bias(tq), w_o, x,
      mod, final_g.reshape(1, D_MODEL))


def _rope_tables(seq):
    pos = jnp.arange(seq, dtype=jnp.int32)
    row = (pos // GRID_W).astype(F32)
    col = (pos % GRID_W).astype(F32)
    axis_dim = HEAD_DIM // 2
    inv_freq = ROPE_THETA ** (-jnp.arange(0, axis_dim, 2, dtype=F32) / axis_dim)
    ang_r = row[:, None] * inv_freq
    ang_c = col[:, None] * inv_freq
    zeros = jnp.zeros_like(ang_r)
    reps = LANES // HEAD_DIM
    cos = jnp.tile(jnp.concatenate(
        [jnp.cos(ang_r), jnp.cos(ang_r), jnp.cos(ang_c), jnp.cos(ang_c)], axis=-1), (1, reps))
    sin_a = jnp.tile(jnp.concatenate(
        [-jnp.sin(ang_r), zeros, -jnp.sin(ang_c), zeros], axis=-1), (1, reps))
    sin_b = jnp.tile(jnp.concatenate(
        [zeros, jnp.sin(ang_r), zeros, jnp.sin(ang_c)], axis=-1), (1, reps))
    return cos, sin_a, sin_b


def kernel(x, c, ctx, c_ctx, w_mod, b_mod, norm_g, w_o, a_w_in, a_lambda_q1, a_lambda_k1,
           a_lambda_q2, a_lambda_k2, a_subln_g, b_w_in, b_sink, final_g):
    bsz, seq, _ = x.shape
    n_ctx = ctx.shape[1]
    depth = w_mod.shape[0]
    tables = _rope_tables(seq)
    ctx_rows = PROJ_ROWS
    assert (bsz * n_ctx) % ctx_rows == 0 and ctx_rows % n_ctx == 0 and seq >= ctx_rows
    ctx_tables = tuple(tb[:ctx_rows] for tb in tables)

    act = jnp.concatenate(
        [c, c_ctx[None, :], jnp.zeros((MOD_ROWS - bsz - 1, D_MODEL), F32)], axis=0)
    mods = _mod_call(act, w_mod, b_mod)

    kv_w = B_KV_HEADS * HEAD_DIM
    for i in range(depth):
        last = i == depth - 1
        j = i // 2
        mod_x = mods[i, :bsz].reshape(bsz, 1, 3 * D_MODEL)
        mod_c = mods[i, bsz:bsz + 1].reshape(1, 1, 3 * D_MODEL)

        def ctx_proj(ctx, g, w, groups, name, mod_c=mod_c):
            outs = _proj_call(ctx.reshape(-1, ctx_rows, D_MODEL), mod_c, g, w, ctx_tables,
                              groups, False, ctx_rows, name)
            return [o.reshape(bsz, n_ctx, D_MODEL) for o in outs]

        w_out = w_o[i].astype(BF16)
        if i % 2 == 0:
            w = a_w_in[j].astype(BF16)
            groups = tuple((kind, n * D_MODEL) for n, kind in enumerate("qkvg"))
            qx, kx, vx, gx = _proj_call(x, mod_x, norm_g[i], w, tables, groups, True,
                                        PROJ_ROWS, "proj_a_x")
            qc, kc, vc, gc = ctx_proj(ctx, norm_g[i], w, groups, "proj_a_ctx")
            lam_init = 0.8 - 0.6 * math.exp(-0.3 * i)
            lams = (a_lambda_q1[j], a_lambda_k1[j], a_lambda_q2[j], a_lambda_k2[j])
            x = _diff_attn_call(qx, kc, vc, kx, vx, gx, lams, a_subln_g[j], lam_init, w_out,
                                x, mod_x, final_g, last, DIFF_ROWS, "diff_attn_x")
            if not last:
                ctx = _diff_attn_call(qc, kc, vc, None, None, gc, lams, a_subln_g[j],
                                      lam_init, w_out, ctx, mod_c, final_g, False, n_ctx,
                                      "diff_attn_ctx")
        else:
            w = b_w_in[j].astype(BF16)
            groups = (("q", 0), ("kv", D_MODEL), ("g", D_MODEL + 2 * kv_w))
            qx, kx, vx, gx = _proj_call(x, mod_x, norm_g[i], w, tables, groups, True,
                                        PROJ_ROWS, "proj_b_x")
            if last:
                kc, vc = ctx_proj(ctx, norm_g[i], w, (("kv", D_MODEL),), "proj_b_ctx")
                x = _win_attn_call(b_sink[j], qx, kc, vc, kx, vx, gx, w_out, x, mod_x,
                                   final_g, True)
            else:
                raise NotImplementedError("context output of a windowed layer")
    return x
```

```python
import functools
import math

import jax
import jax.numpy as jnp
from jax import lax
from jax.experimental import pallas as pl
from jax.experimental.pallas import tpu as pltpu

D_MODEL = 1024
GRID_W = 64
HEAD_DIM = 64
LANES = 128
A_HEADS = D_MODEL // (2 * HEAD_DIM)
B_HEADS = D_MODEL // HEAD_DIM
B_KV_HEADS = 4
B_GROUP = B_HEADS // B_KV_HEADS
B_SLAB = B_GROUP * HEAD_DIM
WINDOW = 128
Q_BLOCK = 128
BAND = Q_BLOCK + 2 * WINDOW

PROJ_ROWS = 1024
DIFF_ROWS = 512
WIN_TILES_PER_STEP = 4
ROPE_THETA = 10000.0
NORM_EPS = 1e-6
SUBLN_EPS = 1e-5
NEG_INF = -1e30
ATTN_SCALE = HEAD_DIM ** -0.5
LOG2E = math.log2(math.e)
MOD_ROWS = 40
VMEM_LIMIT = 56 * 1024 * 1024
VMEM_LIMIT_LARGE = 60 * 1024 * 1024

F32 = jnp.float32
BF16 = jnp.bfloat16
NT_DIMS = (((1,), (1,)), ((), ()))


def _params(n_grid, vmem_limit=VMEM_LIMIT):
    return pltpu.CompilerParams(
        dimension_semantics=("arbitrary",) * n_grid,
        vmem_limit_bytes=vmem_limit)


def _silu(x):
    return x * (1.0 / (1.0 + jnp.exp(-x)))


def _mod_kernel(act_ref, w_ref, b_ref, out_ref):
    a = _silu(act_ref[...])
    out_ref[...] = jnp.dot(a, w_ref[...], preferred_element_type=F32,
                           precision=lax.Precision.HIGHEST) + b_ref[...]


def _mod_call(act, w_mod, b_mod):
    depth = w_mod.shape[0]
    n_blk = w_mod.shape[2] // D_MODEL
    return pl.pallas_call(
        _mod_kernel,
        grid=(depth, n_blk),
        in_specs=[
            pl.BlockSpec((MOD_ROWS, D_MODEL), lambda l, n: (0, 0)),
            pl.BlockSpec((None, D_MODEL, D_MODEL), lambda l, n: (l, 0, n)),
            pl.BlockSpec((None, 1, D_MODEL), lambda l, n: (l, 0, n)),
        ],
        out_specs=pl.BlockSpec((None, MOD_ROWS, D_MODEL), lambda l, n: (l, 0, n)),
        out_shape=jax.ShapeDtypeStruct((depth, MOD_ROWS, 3 * D_MODEL), F32),
        compiler_params=_params(2),
        name="mod_vectors",
    )(act, w_mod, b_mod.reshape(depth, 1, 3 * D_MODEL))


def _rope_block(r, cos, sin_a, sin_b):
    return (r * cos + pltpu.roll(r, LANES - 16, 1) * sin_a
            + pltpu.roll(r, 16, 1) * sin_b)


def _replicate_kv_lanes(r):
    lane = lax.broadcasted_iota(jnp.int32, (r.shape[0], LANES), 1)
    slabs = []
    for p in range(r.shape[1] // LANES):
        v = r[:, p * LANES:(p + 1) * LANES]
        swapped = pltpu.roll(v, HEAD_DIM, 1)
        first = jnp.where(lane < HEAD_DIM, v, swapped)
        second = jnp.where(lane < HEAD_DIM, swapped, v)
        slabs += [first] * (B_GROUP // 2) + [second] * (B_GROUP // 2)
    return jnp.concatenate(slabs, axis=1)


def _proj_kernel(x_ref, mod_ref, g_ref, w_ref, cos_ref, sa_ref, sb_ref, *out_refs,
                 groups, rope):
    x = x_ref[...]
    ms = jnp.mean(x * x, axis=-1, keepdims=True)
    y = x * lax.rsqrt(ms + NORM_EPS) * g_ref[...]
    shift = mod_ref[:, 0:D_MODEL]
    scale = mod_ref[:, D_MODEL:2 * D_MODEL]
    h = (y * (1.0 + scale) + shift).astype(BF16)

    def rope_cols(r):
        if not rope:
            return r
        cos, sin_a, sin_b = cos_ref[...], sa_ref[...], sb_ref[...]
        return jnp.concatenate(
            [_rope_block(r[:, hb * LANES:(hb + 1) * LANES], cos, sin_a, sin_b)
             for hb in range(r.shape[1] // LANES)], axis=1)

    outs = list(out_refs)
    for kind, c0 in groups:
        if kind == "kv":
            kv_w = B_KV_HEADS * HEAD_DIM
            r = jnp.dot(h, w_ref[:, c0:c0 + 2 * kv_w], preferred_element_type=F32)
            for part in (rope_cols(r[:, :kv_w]), r[:, kv_w:]):
                outs.pop(0)[...] = _replicate_kv_lanes(part).astype(BF16)
            continue
        r = jnp.dot(h, w_ref[:, c0:c0 + D_MODEL], preferred_element_type=F32)
        if kind == "q":
            r = r * (ATTN_SCALE * LOG2E)
        if kind in ("q", "k"):
            r = rope_cols(r)
        outs.pop(0)[...] = r.astype(BF16)


def _proj_call(x, mod, norm_g, w, tables, groups, rope, tm, name):
    bsz, t, _ = x.shape
    per_batch_mod = mod.shape[0] != 1
    mod_map = (lambda b, i: (b, 0, 0)) if per_batch_mod else (lambda b, i: (0, 0, 0))
    n_out = sum(2 if kind == "kv" else 1 for kind, _ in groups)
    return pl.pallas_call(
        functools.partial(_proj_kernel, groups=groups, rope=rope),
        grid=(bsz, t // tm),
        in_specs=[
            pl.BlockSpec((None, tm, D_MODEL), lambda b, i: (b, i, 0)),
            pl.BlockSpec((None, 1, 3 * D_MODEL), mod_map),
            pl.BlockSpec((1, D_MODEL), lambda b, i: (0, 0)),
            pl.BlockSpec(w.shape, lambda b, i: (0, 0)),
            pl.BlockSpec((tm, LANES), lambda b, i: (i, 0)),
            pl.BlockSpec((tm, LANES), lambda b, i: (i, 0)),
            pl.BlockSpec((tm, LANES), lambda b, i: (i, 0)),
        ],
        out_specs=[pl.BlockSpec((None, tm, D_MODEL), lambda b, i: (b, i, 0))] * n_out,
        out_shape=[jax.ShapeDtypeStruct((bsz, t, D_MODEL), BF16)] * n_out,
        compiler_params=_params(2),
        name=name,
    )(x, mod, norm_g.reshape(1, D_MODEL), w, *tables)


def _diff_scores(q_ref, kc_ref, kx_ref, h):
    sl = slice(h * LANES, (h + 1) * LANES)
    q = q_ref[:, sl]
    lane = lax.broadcasted_iota(jnp.int32, q.shape, 1)
    zero = jnp.zeros_like(q)
    qs = jnp.concatenate([jnp.where(lane < HEAD_DIM, q, zero),
                          jnp.where(lane >= HEAD_DIM, q, zero)], axis=0)
    s_c = lax.dot_general(qs, kc_ref[:, sl], NT_DIMS, preferred_element_type=F32)
    s_x = None
    if kx_ref is not None:
        s_x = lax.dot_general(qs, kx_ref[:, sl], NT_DIMS, preferred_element_type=F32)
    return s_c, s_x


def _out_proj_residual(o_scr, w_ref, x_ref, mod_ref, fg_ref, out_ref, final):
    upd = jnp.dot(o_scr[...], w_ref[...], preferred_element_type=F32)
    x = x_ref[...] + mod_ref[:, 2 * D_MODEL:3 * D_MODEL] * upd
    if final:
        ms = jnp.mean(x * x, axis=-1, keepdims=True)
        x = x * lax.rsqrt(ms + NORM_EPS) * fg_ref[...]
    out_ref[...] = x


def _diff_attn_kernel(*refs, with_latent, lam_init, final):
    if with_latent:
        (q_ref, kc_ref, vc_ref, kx_ref, vx_ref, gate_ref, lq1_ref, lk1_ref, lq2_ref,
         lk2_ref, sg_ref, w_ref, x_ref, mod_ref, fg_ref, out_ref, o_scr) = refs
    else:
        (q_ref, kc_ref, vc_ref, gate_ref, lq1_ref, lk1_ref, lq2_ref, lk2_ref, sg_ref,
         w_ref, x_ref, mod_ref, fg_ref, out_ref, o_scr) = refs
        kx_ref = vx_ref = None
    tq = q_ref.shape[0]
    lam = (jnp.exp(jnp.sum(lq1_ref[...] * lk1_ref[...], axis=-1, keepdims=True))
           - jnp.exp(jnp.sum(lq2_ref[...] * lk2_ref[...], axis=-1, keepdims=True))
           + lam_init)
    scores = _diff_scores(q_ref, kc_ref, kx_ref, 0)
    for h in range(A_HEADS):
        sl = slice(h * LANES, (h + 1) * LANES)
        s_c, s_x = scores
        if h + 1 < A_HEADS:
            scores = _diff_scores(q_ref, kc_ref, kx_ref, h + 1)
        m = jnp.max(s_c, axis=-1, keepdims=True)
        if with_latent:
            m = jnp.maximum(m, jnp.max(s_x, axis=-1, keepdims=True))
        ones = jnp.ones((vc_ref.shape[0], LANES), BF16)
        p_c = jnp.exp2(s_c - m).astype(BF16)
        oa = jnp.dot(p_c, jnp.concatenate([vc_ref[:, sl], ones], axis=1),
                     preferred_element_type=F32)
        if with_latent:
            ones = jnp.ones((vx_ref.shape[0], LANES), BF16)
            p_x = jnp.exp2(s_x - m).astype(BF16)
            oa = oa + jnp.dot(p_x, jnp.concatenate([vx_ref[:, sl], ones], axis=1),
                              preferred_element_type=F32)
        on = oa[:, :LANES] * (1.0 / oa[:, LANES:])
        o = on[:tq] - lam * on[tq:]
        ms = jnp.mean(o * o, axis=-1, keepdims=True)
        o = o * lax.rsqrt(ms + SUBLN_EPS) * sg_ref[...] * (1.0 - lam_init)
        o_scr[:, sl] = (o * _silu(gate_ref[:, sl].astype(F32))).astype(BF16)
    _out_proj_residual(o_scr, w_ref, x_ref, mod_ref, fg_ref, out_ref, final)


def _mod_spec(mod):
    if mod.shape[0] != 1:
        return pl.BlockSpec((None, 1, 3 * D_MODEL), lambda b, i: (b, 0, 0))
    return pl.BlockSpec((None, 1, 3 * D_MODEL), lambda b, i: (0, 0, 0))


def _diff_attn_call(q, kc, vc, kx, vx, gate, lams, subln_g, lam_init, w_o, x, mod, final_g,
                    final, tq, name):
    bsz, t, _ = q.shape
    n_ctx = kc.shape[1]
    with_latent = kx is not None
    full_blk = lambda rows: pl.BlockSpec((None, rows, D_MODEL), lambda b, i: (b, 0, 0))
    tile_blk = pl.BlockSpec((None, tq, D_MODEL), lambda b, i: (b, i, 0))
    const_blk = lambda r, n: pl.BlockSpec((r, n), lambda b, i: (0, 0))
    in_specs = [tile_blk, full_blk(n_ctx), full_blk(n_ctx)]
    args = [q, kc, vc]
    if with_latent:
        in_specs += [full_blk(kx.shape[1]), full_blk(kx.shape[1])]
        args += [kx, vx]
    in_specs += ([tile_blk] + [const_blk(1, HEAD_DIM)] * 4 + [const_blk(1, 2 * HEAD_DIM)]
                 + [const_blk(D_MODEL, D_MODEL), tile_blk, _mod_spec(mod),
                    const_blk(1, D_MODEL)])
    args += ([gate] + [v.reshape(1, HEAD_DIM) for v in lams] + [subln_g.reshape(1, -1)]
             + [w_o, x, mod, final_g.reshape(1, D_MODEL)])
    return pl.pallas_call(
        functools.partial(_diff_attn_kernel, with_latent=with_latent, lam_init=lam_init,
                          final=final),
        grid=(bsz, t // tq),
        in_specs=in_specs,
        out_specs=tile_blk,
        out_shape=jax.ShapeDtypeStruct((bsz, t, D_MODEL), F32),
        scratch_shapes=[pltpu.VMEM((tq, D_MODEL), BF16)],
        compiler_params=_params(2, VMEM_LIMIT_LARGE),
        name=name,
    )(*args)


def _win_attn_kernel(sink_ref, zero_ref, q_ref, kc_ref, vc_ref, kx_ref, vx_ref, gate_ref,
                     bias_ref, w_ref, x_ref, mod_ref, fg_ref, out_ref, o_scr, s_a, s_b, *,
                     final):
    tq = Q_BLOCK
    tiles = q_ref.shape[0] // tq
    seq = kx_ref.shape[0]
    n_tiles = seq // tq
    n_ctx = kc_ref.shape[0]
    rows = B_GROUP * tq
    row1 = lax.broadcasted_iota(jnp.int32, (rows, 1), 0)
    lane_grp = lax.broadcasted_iota(jnp.int32, (tq, B_SLAB), 1) // HEAD_DIM
    units = [(t, j) for j in range(B_KV_HEADS) for t in range(tiles)]
    ctx_scores = {}
    z = zero_ref[0]
    slots = (s_a, s_b)

    def window_start(t):
        ti = pl.program_id(1) * tiles + t
        kstart = pl.multiple_of(jnp.clip(ti * tq - WINDOW, 0, seq - BAND), WINDOW)
        return ti, kstart

    def stacked_q(t, j):
        qj = q_ref[t * tq:(t + 1) * tq, j * B_SLAB:(j + 1) * B_SLAB]
        zero = jnp.zeros_like(qj)
        return jnp.concatenate(
            [jnp.where(lane_grp == g, qj, zero) for g in range(B_GROUP)], axis=0)

    def scores(u):
        t, j = units[u]
        ti, kstart = window_start(t)
        sl = slice(j * B_SLAB, (j + 1) * B_SLAB)
        if t == 0:
            q_all = jnp.concatenate([stacked_q(tt, j) for tt in range(tiles)], axis=0)
            ctx_scores[j] = lax.dot_general(q_all, kc_ref[:, sl], NT_DIMS,
                                            preferred_element_type=F32)
        slots[u % 2][z, :, :n_ctx] = ctx_scores[j][t * rows:(t + 1) * rows]
        bias = bias_ref[jnp.where(ti == 0, 0, jnp.where(ti == n_tiles - 1, 2, 1))]
        bias = jnp.concatenate([bias] * B_GROUP, axis=0)
        slots[u % 2][z, :, n_ctx:] = lax.dot_general(
            stacked_q(t, j), kx_ref[pl.ds(kstart, BAND), sl], NT_DIMS,
            preferred_element_type=F32) + bias

    scores(0)
    for u, (t, j) in enumerate(units):
        if u + 1 < len(units):
            scores(u + 1)
        _, kstart = window_start(t)
        rs = slice(t * tq, (t + 1) * tq)
        sl = slice(j * B_SLAB, (j + 1) * B_SLAB)
        sink = jnp.full((rows, 1), sink_ref[j * B_GROUP + B_GROUP - 1], F32)
        for g in range(B_GROUP - 2, -1, -1):
            sink = jnp.where(row1 < (g + 1) * tq, sink_ref[j * B_GROUP + g], sink)
        sink = sink * LOG2E
        m = jnp.maximum(jnp.max(slots[u % 2][z], axis=-1, keepdims=True), sink)
        e = jnp.exp2(slots[u % 2][z] - m)
        l = jnp.sum(e, axis=-1, keepdims=True) + jnp.exp2(sink - m)
        e = e.astype(BF16)
        o = (jnp.dot(e[:, :n_ctx], vc_ref[:, sl], preferred_element_type=F32)
             + jnp.dot(e[:, n_ctx:], vx_ref[pl.ds(kstart, BAND), sl],
                       preferred_element_type=F32))
        o = o * (1.0 / l)
        o_j = o[(B_GROUP - 1) * tq:]
        for g in range(B_GROUP - 2, -1, -1):
            o_j = jnp.where(lane_grp == g, o[g * tq:(g + 1) * tq], o_j)
        o_scr[rs, sl] = (o_j * _silu(gate_ref[rs, sl].astype(F32))).astype(BF16)
    _out_proj_residual(o_scr, w_ref, x_ref, mod_ref, fg_ref, out_ref, final)


def _band_bias(tq):
    r = jnp.arange(tq, dtype=jnp.int32)[None, :, None]
    c = jnp.arange(BAND, dtype=jnp.int32)[None, None, :]
    off = (jnp.arange(3, dtype=jnp.int32) * WINDOW)[:, None, None]
    return jnp.where(jnp.abs(c - off - r) <= WINDOW, 0.0, NEG_INF).astype(F32)


def _win_attn_call(sink, q, kc, vc, kx, vx, gate, w_o, x, mod, final_g, final):
    bsz, t, _ = q.shape
    n_ctx = kc.shape[1]
    tq = Q_BLOCK
    rows_per_step = WIN_TILES_PER_STEP * tq
    assert t // tq >= 3 and tq == WINDOW and t % rows_per_step == 0
    tile_blk = pl.BlockSpec((None, rows_per_step, D_MODEL), lambda b, i: (b, i, 0))
    full_blk = lambda rows: pl.BlockSpec((None, rows, D_MODEL), lambda b, i: (b, 0, 0))
    bias_blk = pl.BlockSpec((3, tq, BAND), lambda b, i: (0, 0, 0))
    const_blk = lambda r, n: pl.BlockSpec((r, n), lambda b, i: (0, 0))
    return pl.pallas_call(
        functools.partial(_win_attn_kernel, final=final),
        grid=(bsz, t // rows_per_step),
        in_specs=[pl.BlockSpec(memory_space=pltpu.SMEM),
                  pl.BlockSpec(memory_space=pltpu.SMEM), tile_blk, full_blk(n_ctx),
                  full_blk(n_ctx), full_blk(t), full_blk(t), tile_blk, bias_blk,
                  const_blk(D_MODEL, D_MODEL), tile_blk, _mod_spec(mod),
                  const_blk(1, D_MODEL)],
        out_specs=tile_blk,
        out_shape=jax.ShapeDtypeStruct((bsz, t, D_MODEL), F32),
        scratch_shapes=[pltpu.VMEM((rows_per_step, D_MODEL), BF16)]
        + [pltpu.VMEM((1, B_GROUP * tq, n_ctx + BAND), F32)] * 2,
        compiler_params=_params(2, VMEM_LIMIT_LARGE),
        name="win_attn",
    )(sink, jnp.zeros((1,), jnp.int32), q, kc, vc, kx, vx, gate, _band_bias(tq), w_o, x,
      mod, final_g.reshape(1, D_MODEL))


def _rope_tables(seq):
    pos = jnp.arange(seq, dtype=jnp.int32)
    row = (pos // GRID_W).astype(F32)
    col = (pos % GRID_W).astype(F32)
    axis_dim = HEAD_DIM // 2
    inv_freq = ROPE_THETA ** (-jnp.arange(0, axis_dim, 2, dtype=F32) / axis_dim)
    ang_r = row[:, None] * inv_freq
    ang_c = col[:, None] * inv_freq
    zeros = jnp.zeros_like(ang_r)
    reps = LANES // HEAD_DIM
    cos = jnp.tile(jnp.concatenate(
        [jnp.cos(ang_r), jnp.cos(ang_r), jnp.cos(ang_c), jnp.cos(ang_c)], axis=-1), (1, reps))
    sin_a = jnp.tile(jnp.concatenate(
        [-jnp.sin(ang_r), zeros, -jnp.sin(ang_c), zeros], axis=-1), (1, reps))
    sin_b = jnp.tile(jnp.concatenate(
        [zeros, jnp.sin(ang_r), zeros, jnp.sin(ang_c)], axis=-1), (1, reps))
    return cos, sin_a, sin_b


def kernel(x, c, ctx, c_ctx, w_mod, b_mod, norm_g, w_o, a_w_in, a_lambda_q1, a_lambda_k1,
           a_lambda_q2, a_lambda_k2, a_subln_g, b_w_in, b_sink, final_g):
    bsz, seq, _ = x.shape
    n_ctx = ctx.shape[1]
    depth = w_mod.shape[0]
    tables = _rope_tables(seq)
    ctx_rows = PROJ_ROWS
    assert (bsz * n_ctx) % ctx_rows == 0 and ctx_rows % n_ctx == 0 and seq >= ctx_rows
    ctx_tables = tuple(tb[:ctx_rows] for tb in tables)

    act = jnp.concatenate(
        [c, c_ctx[None, :], jnp.zeros((MOD_ROWS - bsz - 1, D_MODEL), F32)], axis=0)
    mods = _mod_call(act, w_mod, b_mod)

    kv_w = B_KV_HEADS * HEAD_DIM
    for i in range(depth):
        last = i == depth - 1
        j = i // 2
        mod_x = mods[i, :bsz].reshape(bsz, 1, 3 * D_MODEL)
        mod_c = mods[i, bsz:bsz + 1].reshape(1, 1, 3 * D_MODEL)

        def ctx_proj(ctx, g, w, groups, name, mod_c=mod_c):
            outs = _proj_call(ctx.reshape(-1, ctx_rows, D_MODEL), mod_c, g, w, ctx_tables,
                              groups, False, ctx_rows, name)
            return [o.reshape(bsz, n_ctx, D_MODEL) for o in outs]

        w_out = w_o[i].astype(BF16)
        if i % 2 == 0:
            w = a_w_in[j].astype(BF16)
            groups = tuple((kind, n * D_MODEL) for n, kind in enumerate("qkvg"))
            qx, kx, vx, gx = _proj_call(x, mod_x, norm_g[i], w, tables, groups, True,
                                        PROJ_ROWS, "proj_a_x")
            qc, kc, vc, gc = ctx_proj(ctx, norm_g[i], w, groups, "proj_a_ctx")
            lam_init = 0.8 - 0.6 * math.exp(-0.3 * i)
            lams = (a_lambda_q1[j], a_lambda_k1[j], a_lambda_q2[j], a_lambda_k2[j])
            x = _diff_attn_call(qx, kc, vc, kx, vx, gx, lams, a_subln_g[j], lam_init, w_out,
                                x, mod_x, final_g, last, DIFF_ROWS, "diff_attn_x")
            if not last:
                ctx = _diff_attn_call(qc, kc, vc, None, None, gc, lams, a_subln_g[j],
                                      lam_init, w_out, ctx, mod_c, final_g, False, n_ctx,
                                      "diff_attn_ctx")
        else:
            w = b_w_in[j].astype(BF16)
            groups = (("q", 0), ("kv", D_MODEL), ("g", D_MODEL + 2 * kv_w))
            qx, kx, vx, gx = _proj_call(x, mod_x, norm_g[i], w, tables, groups, True,
                                        PROJ_ROWS, "proj_b_x")
            if last:
                kc, vc = ctx_proj(ctx, norm_g[i], w, (("kv", D_MODEL),), "proj_b_ctx")
                x = _win_attn_call(b_sink[j], qx, kc, vc, kx, vx, gx, w_out, x, mod_x,
                                   final_g, True)
            else:
                raise NotImplementedError("context output of a windowed layer")
    return x
```
